```python
import jax, jax.numpy as jnp
from jax import lax
import numpy as np

D_MODEL = 1024
BATCH = 16
SEQ = 2048
DEPTH = 4

N_MIXERS = 2
HEAD_DIM = 64
ROPE_THETA = 10000.0
NORM_EPS = 1e-6
Q_BLOCK = 128
NSA_Q_HEADS = D_MODEL // HEAD_DIM
NSA_KV_HEADS = 4
NSA_GROUP = NSA_Q_HEADS // NSA_KV_HEADS
CMP_BLOCK = 32
CMP_STRIDE = 16
CMP_HIDDEN = 4 * HEAD_DIM
SEL_BLOCK = 64
SEL_TOPK = 8
SEL_Q_CHUNK = 32
NSA_WINDOW = 512
NSA_IN = NSA_Q_HEADS * HEAD_DIM + 6 * NSA_KV_HEADS * HEAD_DIM + 3 * NSA_Q_HEADS
SWA_Q_HEADS = D_MODEL // HEAD_DIM
SWA_KV_HEADS = 2
SWA_GROUP = SWA_Q_HEADS // SWA_KV_HEADS
SWA_WINDOW = 128
SWA_IN = SWA_Q_HEADS * HEAD_DIM + 2 * SWA_KV_HEADS * HEAD_DIM
MLP_HIDDEN = 4 * D_MODEL

kernel_name = "hybrid_nsa_swa_sink_sqrelu_adaln"


def rms_norm(x, g):
    xf = x.astype(jnp.float32)
    y = xf * lax.rsqrt(jnp.mean(xf * xf, axis=-1, keepdims=True) + NORM_EPS)
    return (y * g.astype(jnp.float32)).astype(x.dtype)


def modulate(h, shift, scale):
    return h * (1 + scale[:, None, :]) + shift[:, None, :]


def rope_tables(positions):
    inv = 1.0 / (ROPE_THETA ** (jnp.arange(0, HEAD_DIM, 2, dtype=jnp.float32) / HEAD_DIM))
    ang = positions.astype(jnp.float32)[..., None] * inv
    return jnp.cos(ang)[:, :, None, :], jnp.sin(ang)[:, :, None, :]


def apply_rope(x, cos, sin):
    x1, x2 = jnp.split(x.astype(jnp.float32), 2, axis=-1)
    return jnp.concatenate([x1 * cos - x2 * sin, x2 * cos + x1 * sin], axis=-1).astype(x.dtype)


def banded_attention(q, k, v, window, sinks):
    B, S, Hkv, G, dh = q.shape
    span = window + Q_BLOCK
    pad = ((0, 0), (window, 0), (0, 0), (0, 0))
    kp = jnp.pad(k, pad)
    vp = jnp.pad(v, pad)
    scale = dh ** -0.5

    def block(bi):
        s0 = bi * Q_BLOCK
        qb = lax.dynamic_slice_in_dim(q, s0, Q_BLOCK, axis=1)
        kb = lax.dynamic_slice_in_dim(kp, s0, span, axis=1)
        vb = lax.dynamic_slice_in_dim(vp, s0, span, axis=1)
        s = jnp.einsum('bqhgd,bkhd->bhgqk', qb, kb).astype(jnp.float32) * scale
        t = s0 + jnp.arange(Q_BLOCK)
        j = s0 - window + jnp.arange(span)
        dist = t[:, None] - j[None, :]
        mask = (dist >= 0) & (dist < window) & (j[None, :] >= 0)
        s = jnp.where(mask, s, -jnp.inf)
        if sinks is None:
            p = jax.nn.softmax(s, axis=-1)
        else:
            sk = sinks.astype(jnp.float32)[None, :, :, None, None]
            m = jnp.maximum(jnp.max(s, axis=-1, keepdims=True), sk)
            e = jnp.exp(s - m)
            p = e / (jnp.sum(e, axis=-1, keepdims=True) + jnp.exp(sk - m))
        return jnp.einsum('bhgqk,bkhd->bqhgd', p.astype(v.dtype), vb)

    out = lax.map(block, jnp.arange(S // Q_BLOCK))
    return jnp.moveaxis(out, 0, 1).reshape(B, S, Hkv, G, dh)


def compress_tokens(x, pe, w1, b1, w2, b2):
    B, S, Hkv, dh = x.shape
    n_cmp = (S - CMP_BLOCK) // CMP_STRIDE + 1
    idx = np.arange(n_cmp)[:, None] * CMP_STRIDE + np.arange(CMP_BLOCK)[None, :]
    xb = x[:, idx] + pe[None, None, :, None, :]
    xb = jnp.moveaxis(xb, 3, 2).reshape(B, n_cmp, Hkv, CMP_BLOCK * dh)
    return jax.nn.gelu(xb @ w1 + b1) @ w2 + b2


def compressed_attention(q, kc, vc):
    B, S, Hkv, G, dh = q.shape
    n_cmp = kc.shape[1]
    s = jnp.einsum('bshgd,bnhd->bhgsn', q, kc).astype(jnp.float32) * dh ** -0.5
    blk_end = jnp.arange(n_cmp) * CMP_STRIDE + CMP_BLOCK - 1
    mask = blk_end[None, :] <= jnp.arange(S)[:, None]
    s = jnp.where(mask, s, -jnp.inf)
    m = jnp.max(s, axis=-1, keepdims=True)
    m = jnp.where(jnp.isfinite(m), m, 0.0)
    e = jnp.where(mask, jnp.exp(s - m), 0.0)
    den = jnp.sum(e, axis=-1, keepdims=True)
    p = e / jnp.where(den > 0, den, 1.0)
    o = jnp.einsum('bhgsn,bnhd->bshgd', p.astype(vc.dtype), vc)
    return o, p


def select_blocks(p_cmp, S):
    n_cmp = p_cmp.shape[-1]
    n_sel = S // SEL_BLOCK
    cs = np.arange(n_cmp)[:, None] * CMP_STRIDE
    ss = np.arange(n_sel)[None, :] * SEL_BLOCK
    overlap = np.clip(np.minimum(cs + CMP_BLOCK, ss + SEL_BLOCK) - np.maximum(cs, ss), 0, None)
    M = jnp.asarray((overlap / CMP_STRIDE).astype(np.float32))
    imp = jnp.einsum('bhgsn,nj->bhsj', p_cmp, M)
    cur = jnp.arange(S) // SEL_BLOCK
    jj = jnp.arange(n_sel)
    causal = jj[None, :] <= cur[:, None]
    forced = (jj[None, :] == 0) | (jj[None, :] == cur[:, None]) | (jj[None, :] == cur[:, None] - 1)
    imp = jnp.where(forced, jnp.inf, jnp.where(causal, imp, -jnp.inf))
    _, idx = lax.top_k(imp, min(SEL_TOPK, n_sel))
    return idx


def selected_block_attention(q, k, v, sel_idx):
    B, S, Hkv, G, dh = q.shape
    n_sel = S // SEL_BLOCK
    kb = k.reshape(B, n_sel, SEL_BLOCK, Hkv, dh).transpose(0, 3, 1, 2, 4)
    vb = v.reshape(B, n_sel, SEL_BLOCK, Hkv, dh).transpose(0, 3, 1, 2, 4)
    bi = jnp.arange(B)[:, None, None, None]
    hi = jnp.arange(Hkv)[None, :, None, None]
    scale = dh ** -0.5

    def chunk(ci):
        s0 = ci * SEL_Q_CHUNK
        qc = lax.dynamic_slice_in_dim(q, s0, SEL_Q_CHUNK, axis=1)
        ic = lax.dynamic_slice_in_dim(sel_idx, s0, SEL_Q_CHUNK, axis=2)
        kg = kb[bi, hi, ic]
        vg = vb[bi, hi, ic]
        s = jnp.einsum('bqhgd,bhqnld->bhgqnl', qc, kg).astype(jnp.float32) * scale
        t = s0 + jnp.arange(SEL_Q_CHUNK)
        pos = ic[..., None] * SEL_BLOCK + jnp.arange(SEL_BLOCK)
        mask = (pos <= t[None, None, :, None, None])[:, :, None]
        s = jnp.where(mask, s, -jnp.inf)
        n = ic.shape[-1]
        p = jax.nn.softmax(s.reshape(B, Hkv, G, SEL_Q_CHUNK, n * SEL_BLOCK), axis=-1).reshape(s.shape)
        return jnp.einsum('bhgqnl,bhqnld->bqhgd', p.astype(v.dtype), vg)

    out = lax.map(chunk, jnp.arange(S // SEL_Q_CHUNK))
    return jnp.moveaxis(out, 0, 1).reshape(B, S, Hkv, G, dh)


def nsa_mixer(h, w_in, w_out, cmp_pe, phi_w1, phi_b1, phi_w2, phi_b2, cos, sin):
    B, S, _ = h.shape
    qd = NSA_Q_HEADS * HEAD_DIM
    kd = NSA_KV_HEADS * HEAD_DIM
    proj = h @ w_in
    splits = [qd + i * kd for i in range(7)]
    q, kc, vc, ks, vs, kw, vw, g = jnp.split(proj, splits, axis=-1)
    q = apply_rope(q.reshape(B, S, NSA_Q_HEADS, HEAD_DIM), cos, sin)
    q = q.reshape(B, S, NSA_KV_HEADS, NSA_GROUP, HEAD_DIM)
    kv_shape = (B, S, NSA_KV_HEADS, HEAD_DIM)
    kc = apply_rope(kc.reshape(kv_shape), cos, sin)
    ks = apply_rope(ks.reshape(kv_shape), cos, sin)
    kw = apply_rope(kw.reshape(kv_shape), cos, sin)
    vc, vs, vw = vc.reshape(kv_shape), vs.reshape(kv_shape), vw.reshape(kv_shape)
    kcc = compress_tokens(kc, cmp_pe[0], phi_w1[0], phi_b1[0], phi_w2[0], phi_b2[0])
    vcc = compress_tokens(vc, cmp_pe[1], phi_w1[1], phi_b1[1], phi_w2[1], phi_b2[1])
    o_cmp, p_cmp = compressed_attention(q, kcc, vcc)
    sel_idx = select_blocks(p_cmp, S)
    o_sel = selected_block_attention(q, ks, vs, sel_idx)
    o_win = banded_attention(q, kw, vw, NSA_WINDOW, None)
    g = jax.nn.sigmoid(g.astype(jnp.float32)).astype(h.dtype)
    g = g.reshape(B, S, NSA_KV_HEADS, NSA_GROUP, 3)
    o = g[..., 0:1] * o_cmp + g[..., 1:2] * o_sel + g[..., 2:3] * o_win
    return o.reshape(B, S, qd) @ w_out


def swa_sink_mixer(h, w_in, w_out, sinks, cos, sin):
    B, S, _ = h.shape
    qd = SWA_Q_HEADS * HEAD_DIM
    kd = SWA_KV_HEADS * HEAD_DIM
    q, k, v = jnp.split(h @ w_in, [qd, qd + kd], axis=-1)
    q = apply_rope(q.reshape(B, S, SWA_Q_HEADS, HEAD_DIM), cos, sin)
    q = q.reshape(B, S, SWA_KV_HEADS, SWA_GROUP, HEAD_DIM)
    k = apply_rope(k.reshape(B, S, SWA_KV_HEADS, HEAD_DIM), cos, sin)
    v = v.reshape(B, S, SWA_KV_HEADS, HEAD_DIM)
    o = banded_attention(q, k, v, SWA_WINDOW, sinks.reshape(SWA_KV_HEADS, SWA_GROUP))
    return o.reshape(B, S, qd) @ w_out


def squared_relu_mlp(h, w_up, w_down):
    a = jax.nn.relu(h @ w_up)
    return (a * a) @ w_down


def setup_inputs(seed: int = 0) -> dict:
    key = jax.random.key(seed)
    ks = jax.random.split(key, 20)
    n_a = (DEPTH + N_MIXERS - 1) // N_MIXERS
    n_b = DEPTH // N_MIXERS
    nrm = jax.random.normal
    f32 = jnp.float32
    x = nrm(ks[0], (BATCH, SEQ, D_MODEL), f32)
    c = nrm(ks[1], (BATCH, D_MODEL), f32)
    offset = jax.random.randint(ks[2], (BATCH, 1), 0, 4096, dtype=jnp.int32)
    positions = (offset + jnp.arange(SEQ, dtype=jnp.int32)[None, :]).astype(jnp.int32)
    ada_w = nrm(ks[3], (DEPTH, D_MODEL, 6 * D_MODEL), f32) * (0.5 * D_MODEL ** -0.5)
    ada_b = nrm(ks[4], (DEPTH, 6 * D_MODEL), f32) * 0.01
    norm_g = 1.0 + 0.02 * nrm(ks[5], (DEPTH, 4, D_MODEL), f32)
    nsa_w_in = nrm(ks[6], (n_a, D_MODEL, NSA_IN), f32) * D_MODEL ** -0.5
    nsa_w_out = nrm(ks[7], (n_a, NSA_Q_HEADS * HEAD_DIM, D_MODEL), f32) * (NSA_Q_HEADS * HEAD_DIM) ** -0.5
    nsa_cmp_pe = 0.1 * nrm(ks[8], (n_a, 2, CMP_BLOCK, HEAD_DIM), f32)
    nsa_phi_w1 = nrm(ks[9], (n_a, 2, CMP_BLOCK * HEAD_DIM, CMP_HIDDEN), f32) * (CMP_BLOCK * HEAD_DIM) ** -0.5
    nsa_phi_b1 = 0.01 * nrm(ks[10], (n_a, 2, CMP_HIDDEN), f32)
    nsa_phi_w2 = nrm(ks[11], (n_a, 2, CMP_HIDDEN, HEAD_DIM), f32) * CMP_HIDDEN ** -0.5
    nsa_phi_b2 = 0.01 * nrm(ks[12], (n_a, 2, HEAD_DIM), f32)
    swa_w_in = nrm(ks[13], (n_b, D_MODEL, SWA_IN), f32) * D_MODEL ** -0.5
    swa_w_out = nrm(ks[14], (n_b, SWA_Q_HEADS * HEAD_DIM, D_MODEL), f32) * (SWA_Q_HEADS * HEAD_DIM) ** -0.5
    swa_sinks = nrm(ks[15], (n_b, SWA_Q_HEADS), f32)
    mlp_w_up = nrm(ks[16], (DEPTH, D_MODEL, MLP_HIDDEN), f32) * D_MODEL ** -0.5
    mlp_w_down = nrm(ks[17], (DEPTH, MLP_HIDDEN, D_MODEL), f32) * MLP_HIDDEN ** -0.5
    return {"x": x, "c": c, "positions": positions, "ada_w": ada_w, "ada_b": ada_b,
            "norm_g": norm_g, "nsa_w_in": nsa_w_in, "nsa_w_out": nsa_w_out,
            "nsa_cmp_pe": nsa_cmp_pe, "nsa_phi_w1": nsa_phi_w1, "nsa_phi_b1": nsa_phi_b1,
            "nsa_phi_w2": nsa_phi_w2, "nsa_phi_b2": nsa_phi_b2, "swa_w_in": swa_w_in,
            "swa_w_out": swa_w_out, "swa_sinks": swa_sinks, "mlp_w_up": mlp_w_up,
            "mlp_w_down": mlp_w_down}


def reference(x, c, positions, ada_w, ada_b, norm_g, nsa_w_in, nsa_w_out, nsa_cmp_pe,
              nsa_phi_w1, nsa_phi_b1, nsa_phi_w2, nsa_phi_b2, swa_w_in, swa_w_out,
              swa_sinks, mlp_w_up, mlp_w_down):
    cos, sin = rope_tables(positions)
    cond = jax.nn.silu(c)
    for i in range(DEPTH):
        mod = cond @ ada_w[i] + ada_b[i]
        sh1, sc1, g1, sh2, sc2, g2 = jnp.split(mod, 6, axis=-1)
        h = modulate(rms_norm(x, norm_g[i, 0]), sh1, sc1)
        a = i // N_MIXERS
        if i % N_MIXERS == 0:
            y = nsa_mixer(h, nsa_w_in[a], nsa_w_out[a], nsa_cmp_pe[a], nsa_phi_w1[a],
                          nsa_phi_b1[a], nsa_phi_w2[a], nsa_phi_b2[a], cos, sin)
        else:
            y = swa_sink_mixer(h, swa_w_in[a], swa_w_out[a], swa_sinks[a], cos, sin)
        x = x + (1 + g1)[:, None, :] * rms_norm(y, norm_g[i, 1])
        h = modulate(rms_norm(x, norm_g[i, 2]), sh2, sc2)
        y = squared_relu_mlp(h, mlp_w_up[i], mlp_w_down[i])
        x = x + (1 + g2)[:, None, :] * rms_norm(y, norm_g[i, 3])
    return x
```

```python
import functools

import jax
import jax.numpy as jnp
from jax import lax
from jax.experimental import pallas as pl
from jax.experimental.pallas import tpu as pltpu

F32 = jnp.float32
BF16 = jnp.bfloat16

HEAD_DIM = 64
ROPE_THETA = 10000.0
NORM_EPS = 1e-6
ATTN_SCALE = HEAD_DIM ** -0.5
GROUP = 4
SLAB = GROUP * HEAD_DIM
LANES = 128
NSA_KV_HEADS = 4
NSA_WINDOW = 512
CMP_BLOCK = 32
CMP_STRIDE = 16
SEL_BLOCK = 64
SEL_TOPK = 8
SWA_KV_HEADS = 2
SWA_WINDOW = 128
VMEM_LIMIT = 56 * 1024 * 1024
NEG_INF = float("-inf")


def _dot(a, b):
    return jnp.dot(a, b, preferred_element_type=F32)


def _dot_nt(a, b):
    return lax.dot_general(a, b, (((1,), (1,)), ((), ())), preferred_element_type=F32)


def _params(sem):
    return pltpu.CompilerParams(dimension_semantics=sem, vmem_limit_bytes=VMEM_LIMIT)


def _resident(shape, index_map):
    return pl.BlockSpec(shape, index_map, pipeline_mode=pl.Buffered(1))


def _rope_table_kernel(pos_ref, inv_ref, cos_ref, sin_ref):
    ang = pos_ref[...].astype(F32) * inv_ref[...]
    lane = lax.broadcasted_iota(jnp.int32, ang.shape, 1)
    first_half = (lane % HEAD_DIM) < HEAD_DIM // 2
    cos_ref[...] = jnp.cos(ang)
    s = jnp.sin(ang)
    sin_ref[...] = jnp.where(first_half, -s, s)


def _rope_tables(positions):
    t = positions.size
    tm = min(t, 2048)
    inv = 1.0 / (ROPE_THETA ** (jnp.arange(0, HEAD_DIM, 2, dtype=F32) / HEAD_DIM))
    inv_lanes = jnp.tile(inv, LANES // (HEAD_DIM // 2)).reshape(1, LANES)
    return pl.pallas_call(
        _rope_table_kernel,
        grid=(t // tm,),
        in_specs=[pl.BlockSpec((tm, 1), lambda i: (i, 0)),
                  pl.BlockSpec((1, LANES), lambda i: (0, 0))],
        out_specs=[pl.BlockSpec((tm, LANES), lambda i: (i, 0)),
                   pl.BlockSpec((tm, LANES), lambda i: (i, 0))],
        out_shape=[jax.ShapeDtypeStruct((t, LANES), F32)] * 2,
        compiler_params=_params(("arbitrary",)),
        name="rope_tables",
    )(positions.reshape(t, 1), inv_lanes)


def _rope(v, cos, sin_signed):
    lane = lax.broadcasted_iota(jnp.int32, (v.shape[0], LANES), 1)
    first_half = (lane % HEAD_DIM) < HEAD_DIM // 2
    outs = []
    for j in range(v.shape[1] // LANES):
        s = v[:, j * LANES:(j + 1) * LANES]
        partner = jnp.where(first_half,
                            pltpu.roll(s, LANES - HEAD_DIM // 2, 1),
                            pltpu.roll(s, HEAD_DIM // 2, 1))
        outs.append(s * cos + partner * sin_signed)
    return outs[0] if len(outs) == 1 else jnp.concatenate(outs, axis=1)


def _mod_kernel(c_ref, w_ref, b_ref, o_ref):
    c = c_ref[...]
    cond = (c * jax.nn.sigmoid(c)).astype(BF16)
    o_ref[0] = _dot(cond, w_ref[0].astype(BF16)) + b_ref[0]


def _modulation(c, ada_w, ada_b):
    depth, d, n = ada_w.shape
    b = c.shape[0]
    tn = 1024
    return pl.pallas_call(
        _mod_kernel,
        grid=(depth, n // tn),
        in_specs=[pl.BlockSpec((b, d), lambda l, j: (0, 0)),
                  pl.BlockSpec((1, d, tn), lambda l, j: (l, 0, j)),
                  pl.BlockSpec((1, 1, tn), lambda l, j: (l, 0, j))],
        out_specs=pl.BlockSpec((1, b, tn), lambda l, j: (l, 0, j)),
        out_shape=jax.ShapeDtypeStruct((depth, b, n), F32),
        compiler_params=_params(("arbitrary", "arbitrary")),
        name="adaln_modulation",
    )(c, ada_w, ada_b.reshape(depth, 1, n))


def _rms(x, g):
    ms = jnp.mean(x * x, axis=-1, keepdims=True)
    return x * lax.rsqrt(ms + NORM_EPS) * g


def _nsa_pre_kernel(x_ref, g_ref, sh_ref, sc_ref, wq_ref, wkv_ref, wg_ref, cos_ref, sin_ref,
                    q_ref, kvc_ref, kvb_ref, gate_ref):
    h = _rms(x_ref[...], g_ref[0]) * (1.0 + sc_ref[0]) + sh_ref[0]
    hb = h.astype(BF16)
    cos = cos_ref[...]
    sin = sin_ref[...]
    q = _rope(_dot(hb, wq_ref[...]), cos, sin) * ATTN_SCALE
    q_ref[...] = q.astype(BF16)
    kv = _dot(hb, wkv_ref[...])
    kd = kvc_ref.shape[1] // 2
    kc = _rope(kv[:, 0:kd], cos, sin)
    kvc_ref[...] = jnp.concatenate([kc, kv[:, kd:2 * kd]], axis=1)
    ks = _rope(kv[:, 2 * kd:3 * kd], cos, sin)
    kw = _rope(kv[:, 4 * kd:5 * kd], cos, sin)
    kvb_ref[...] = jnp.concatenate(
        [ks, kv[:, 3 * kd:4 * kd], kw, kv[:, 5 * kd:6 * kd]], axis=1).astype(BF16)
    gate_ref[...] = jax.nn.sigmoid(_dot(hb, wg_ref[...]))


def _swa_pre_kernel(x_ref, g_ref, sh_ref, sc_ref, wq_ref, wkv_ref, cos_ref, sin_ref,
                    q_ref, kvb_ref):
    h = _rms(x_ref[...], g_ref[0]) * (1.0 + sc_ref[0]) + sh_ref[0]
    hb = h.astype(BF16)
    cos = cos_ref[...]
    sin = sin_ref[...]
    q = _rope(_dot(hb, wq_ref[...]), cos, sin) * ATTN_SCALE
    q_ref[...] = q.astype(BF16)
    kv = _dot(hb, wkv_ref[...])
    kd = kvb_ref.shape[1] // 2
    k = _rope(kv[:, 0:kd], cos, sin)
    kvb_ref[...] = jnp.concatenate([k, kv[:, kd:]], axis=1).astype(BF16)


def _pre_mixer(kind, x2, mod_rows, norm_rows, layer, batch, seq, weights, cos_t, sin_t):
    t, d = x2.shape
    tm = min(seq, 512)
    nt = seq // tm
    row = lambda b, i: (b * nt + i, 0)
    mod_row = lambda part: pl.BlockSpec((1, 1, d), lambda b, i: ((layer * batch + b) * 6 + part, 0, 0))
    common_in = [pl.BlockSpec((tm, d), row),
                 pl.BlockSpec((1, 1, d), lambda b, i: (layer * 4 + 0, 0, 0)),
                 mod_row(0), mod_row(1)]
    tab = [pl.BlockSpec((tm, LANES), row), pl.BlockSpec((tm, LANES), row)]
    if kind == "nsa":
        wq, wkv, wg = weights
        kd = NSA_KV_HEADS * HEAD_DIM
        return pl.pallas_call(
            _nsa_pre_kernel,
            grid=(batch, nt),
            in_specs=common_in + [_resident(wq.shape, lambda b, i: (0, 0)),
                                  _resident(wkv.shape, lambda b, i: (0, 0)),
                                  _resident(wg.shape, lambda b, i: (0, 0))] + tab,
            out_specs=[pl.BlockSpec((tm, wq.shape[1]), row),
                       pl.BlockSpec((tm, 2 * kd), row),
                       pl.BlockSpec((tm, 4 * kd), row),
                       pl.BlockSpec((tm, LANES), row)],
            out_shape=[jax.ShapeDtypeStruct((t, wq.shape[1]), BF16),
                       jax.ShapeDtypeStruct((t, 2 * kd), F32),
                       jax.ShapeDtypeStruct((t, 4 * kd), BF16),
                       jax.ShapeDtypeStruct((t, LANES), F32)],
            compiler_params=_params(("arbitrary", "arbitrary")),
            name="nsa_pre",
        )(x2, norm_rows, mod_rows, mod_rows, wq, wkv, wg, cos_t, sin_t)
    wq, wkv = weights
    kd = SWA_KV_HEADS * HEAD_DIM
    return pl.pallas_call(
        _swa_pre_kernel,
        grid=(batch, nt),
        in_specs=common_in + [_resident(wq.shape, lambda b, i: (0, 0)),
                              _resident(wkv.shape, lambda b, i: (0, 0))] + tab,
        out_specs=[pl.BlockSpec((tm, wq.shape[1]), row),
                   pl.BlockSpec((tm, 2 * kd), row)],
        out_shape=[jax.ShapeDtypeStruct((t, wq.shape[1]), BF16),
                   jax.ShapeDtypeStruct((t, 2 * kd), BF16)],
        compiler_params=_params(("arbitrary", "arbitrary")),
        name="swa_pre",
    )(x2, norm_rows, mod_rows, mod_rows, wq, wkv, cos_t, sin_t)


def _compress_kernel(x_ref, pe_ref, w1_ref, b1_ref, w2_ref, b2_ref, o_ref):
    nc = x_ref.shape[3]
    ncp = o_ref.shape[3]
    pe_lo = pe_ref[0, 0]
    pe_hi = pe_ref[0, 1]
    for h in range(x_ref.shape[2]):
        xh = x_ref[0, 0, h]
        lo = _dot((xh + pe_lo).astype(BF16), w1_ref[0, 0])
        hi = _dot((xh + pe_hi).astype(BF16), w1_ref[0, 1])
        pre = lo + pltpu.roll(hi, nc - 1, 0) + b1_ref[0]
        hid = jax.nn.gelu(pre, approximate=True).astype(BF16)
        out = _dot(hid, w2_ref[0]) + b2_ref[0]
        if ncp > nc:
            out = jnp.concatenate([out, jnp.zeros((ncp - nc, HEAD_DIM), F32)], axis=0)
        o_ref[0, 0, h] = out


def _compress(kvc, batch, seq, pe, w1, b1, w2, b2, ncp):
    nc = seq // CMP_STRIDE
    hk = NSA_KV_HEADS
    chunk = CMP_STRIDE * HEAD_DIM
    xr = kvc.reshape(batch, nc, CMP_STRIDE, 2, hk, HEAD_DIM).transpose(0, 3, 4, 1, 2, 5)
    xr = xr.reshape(batch, 2, hk, nc, chunk)
    hidden = w1.shape[-1]
    return pl.pallas_call(
        _compress_kernel,
        grid=(batch, 2),
        in_specs=[pl.BlockSpec((1, 1, hk, nc, chunk), lambda b, j: (b, j, 0, 0, 0)),
                  pl.BlockSpec((1, 2, 1, chunk), lambda b, j: (j, 0, 0, 0)),
                  pl.BlockSpec((1, 2, chunk, hidden), lambda b, j: (j, 0, 0, 0)),
                  pl.BlockSpec((1, 1, hidden), lambda b, j: (j, 0, 0)),
                  pl.BlockSpec((1, hidden, HEAD_DIM), lambda b, j: (j, 0, 0)),
                  pl.BlockSpec((1, 1, HEAD_DIM), lambda b, j: (j, 0, 0))],
        out_specs=pl.BlockSpec((1, 1, hk, ncp, HEAD_DIM), lambda b, j: (b, j, 0, 0, 0)),
        out_shape=jax.ShapeDtypeStruct((batch, 2, hk, ncp, HEAD_DIM), F32),
        compiler_params=_params(("arbitrary", "arbitrary")),
        name="nsa_compress",
    )(xr, pe.reshape(2, 2, 1, chunk), w1.astype(BF16).reshape(2, 2, chunk, hidden),
      b1.reshape(2, 1, hidden), w2.astype(BF16), b2.reshape(2, 1, HEAD_DIM))


def _tile_lanes(a, n):
    return jnp.concatenate([a] * n, axis=1)


def _stack_queries(qt):
    lane_head = lax.broadcasted_iota(jnp.int32, qt.shape, 1) // HEAD_DIM
    keep = [jnp.where(lane_head == g, 1.0, 0.0).astype(BF16) for g in range(GROUP)]
    return jnp.concatenate([qt * k for k in keep], axis=0)


def _head_select(n_rows, n_cols, first_row, replicate):
    r = lax.broadcasted_iota(jnp.int32, (n_rows, n_cols), 0)
    c = lax.broadcasted_iota(jnp.int32, (n_rows, n_cols), 1)
    if replicate:
        hit = r == first_row + c % HEAD_DIM
    else:
        hit = (c < HEAD_DIM) & (r == first_row + c)
    return jnp.where(hit, 1.0, 0.0).astype(BF16)


def _build_kv_tiles(src_ref, k_col, v_col, n_col, first_row, krep_ref, vt_ref):
    nt, tq, _ = krep_ref.shape
    rk = _head_select(n_col, SLAB, first_row, True)
    rv = _head_select(n_col, LANES, first_row, False)
    for t in range(nt):
        krep_ref[t] = _dot(src_ref[t * tq:(t + 1) * tq, k_col:k_col + n_col], rk).astype(BF16)
        vt = _dot(src_ref[t * tq:(t + 1) * tq, v_col:v_col + n_col], rv).T
        vt_ref[t] = vt[0:HEAD_DIM].astype(BF16)


def _unstack_output(o_t, tq):
    halves = []
    for p in range(GROUP // 2):
        pair = jnp.concatenate([o_t[:, (2 * p) * tq:(2 * p + 1) * tq],
                                o_t[:, (2 * p + 1) * tq:(2 * p + 2) * tq]], axis=0)
        halves.append(pair.T)
    return jnp.concatenate(halves, axis=1)


def _band_bias(q_pos, k_pos, window):
    d = q_pos - k_pos
    return jnp.where((d >= 0) & (d < window), 0.0, NEG_INF)


def _online_step(carry, s, v_t):
    m, l, acc = carry
    m_new = jnp.maximum(m, jnp.max(s, axis=0, keepdims=True))
    alpha = jnp.exp(m - m_new)
    p = jnp.exp(s - m_new)
    l = alpha * l + jnp.sum(p, axis=0, keepdims=True)
    acc = alpha * acc + _dot(v_t, p.astype(BF16))
    return m_new, l, acc


def _window_branch(q_stack, krep_ref, vt_ref, qi, tq, window, first):
    q_pos = qi * tq + lax.broadcasted_iota(jnp.int32, (1, tq), 1)
    n_prev = -(-window // tq)
    keep = min(window, tq)

    def body(c, carry):
        kt = qi - c
        k_tile = krep_ref[kt, tq - keep:tq, :]
        v_tile = vt_ref[kt, :, tq - keep:tq]
        k_pos = kt * tq + (tq - keep) + lax.broadcasted_iota(jnp.int32, (keep, 1), 0)
        s = _dot_nt(k_tile, q_stack) + _tile_lanes(_band_bias(q_pos, k_pos, window), GROUP)
        return _online_step(carry, s, v_tile)

    return lax.fori_loop(1, jnp.minimum(n_prev, qi) + 1, body, first)


def _nsa_attn_kernel(q_ref, kvb_ref, kvcc_ref, gate_ref, o_ref,
                     ks_ref, vs_ref, kw_ref, vw_ref, kc_ref, vc_ref, sel_ref, gt_ref,
                     *, n_cmp, n_sel):
    h = pl.program_id(1)
    qi = pl.program_id(2)
    tq = q_ref.shape[0]
    ncp = kc_ref.shape[0]
    nsp = sel_ref.shape[0]
    kd = NSA_KV_HEADS * HEAD_DIM
    blocks_per_tile = tq // SEL_BLOCK

    @pl.when(qi == 0)
    def _():
        first = h * HEAD_DIM
        _build_kv_tiles(kvb_ref, 0, kd, kd, first, ks_ref, vs_ref)
        _build_kv_tiles(kvb_ref, 2 * kd, 3 * kd, kd, first, kw_ref, vw_ref)
        kc_ref[...] = _dot(kvcc_ref[0, 0, 0].astype(BF16), _head_select(HEAD_DIM, SLAB, 0, True)).astype(BF16)
        vct = _dot(kvcc_ref[0, 1, 0].astype(BF16), _head_select(HEAD_DIM, LANES, 0, False)).T
        vc_ref[...] = vct[0:HEAD_DIM].astype(BF16)

    s0 = qi * tq
    q_stack = _stack_queries(q_ref[...])
    q_pos = s0 + lax.broadcasted_iota(jnp.int32, (1, tq), 1)
    q_pos4 = _tile_lanes(q_pos, GROUP)

    sc = _dot_nt(kc_ref[...], q_stack)
    n_idx = lax.broadcasted_iota(jnp.int32, (ncp, 1), 0)
    valid = (n_idx * CMP_STRIDE + (CMP_BLOCK - 1) <= q_pos4) & (n_idx < n_cmp)
    sc = jnp.where(valid, sc, NEG_INF)
    mc = jnp.max(sc, axis=0, keepdims=True)
    mc = jnp.where(mc == NEG_INF, 0.0, mc)
    ec = jnp.where(valid, jnp.exp(sc - mc), 0.0)
    den = jnp.sum(ec, axis=0, keepdims=True)
    pc = (ec / jnp.where(den > 0, den, 1.0)).astype(BF16)
    o_cmp = _dot(vc_ref[...], pc)

    j_blk = lax.broadcasted_iota(jnp.int32, (nsp, ncp), 0)
    n_blk = lax.broadcasted_iota(jnp.int32, (nsp, ncp), 1)
    overlap = (jnp.minimum(n_blk * CMP_STRIDE + CMP_BLOCK, j_blk * SEL_BLOCK + SEL_BLOCK)
               - jnp.maximum(n_blk * CMP_STRIDE, j_blk * SEL_BLOCK))
    overlap = jnp.where((n_blk < n_cmp) & (j_blk < n_sel), jnp.maximum(overlap, 0), 0)
    weight = (overlap.astype(F32) * (1.0 / CMP_STRIDE)).astype(BF16)
    imp4 = _dot(weight, pc)
    imp = imp4[:, 0:tq]
    for g in range(1, GROUP):
        imp = imp + imp4[:, g * tq:(g + 1) * tq]
    j_row = lax.broadcasted_iota(jnp.int32, (nsp, tq), 0).astype(F32)
    cur = jnp.right_shift(s0 + lax.broadcasted_iota(jnp.int32, (nsp, tq), 1), 6).astype(F32)
    forced = (j_row == 0) | (j_row == cur) | (j_row == cur - 1)
    score = jnp.where(forced, jnp.inf, jnp.where(j_row <= cur, imp, NEG_INF))
    chosen = jnp.zeros((nsp, tq), F32)
    for _ in range(min(SEL_TOPK, n_sel)):
        best = jnp.max(score, axis=0, keepdims=True)
        idx = jnp.min(jnp.where(score == best, j_row, float(nsp)), axis=0, keepdims=True)
        pick = j_row == idx
        chosen = jnp.where(pick, 1.0, chosen)
        score = jnp.where(pick, NEG_INF, score)
    sel_ref[...] = chosen

    def block_bias(kt):
        rows = [jnp.broadcast_to(sel_ref[pl.ds(kt * blocks_per_tile + i, 1), :], (SEL_BLOCK, tq))
                for i in range(blocks_per_tile)]
        return jnp.where(jnp.concatenate(rows, axis=0) > 0.5, 0.0, NEG_INF)

    k_pos = s0 + lax.broadcasted_iota(jnp.int32, (tq, 1), 0)
    causal_bias = jnp.where(k_pos <= q_pos, 0.0, NEG_INF)

    def first_tile(k_tile, v_tile, bias):
        s = _dot_nt(k_tile, q_stack) + _tile_lanes(bias, GROUP)
        m = jnp.max(s, axis=0, keepdims=True)
        p = jnp.exp(s - m)
        return m, jnp.sum(p, axis=0, keepdims=True), _dot(v_tile, p.astype(BF16))

    first = first_tile(ks_ref[qi], vs_ref[qi], causal_bias + block_bias(qi))

    def sel_body(kt, carry):
        s = _dot_nt(ks_ref[kt], q_stack) + _tile_lanes(block_bias(kt), GROUP)
        return _online_step(carry, s, vs_ref[kt])

    _, l_sel, acc_sel = lax.fori_loop(0, qi, sel_body, first)

    first = first_tile(kw_ref[qi], vw_ref[qi], _band_bias(q_pos, k_pos, NSA_WINDOW))
    _, l_win, acc_win = _window_branch(q_stack, kw_ref, vw_ref, qi, tq, NSA_WINDOW, first)

    gt_ref[...] = gate_ref[...].T

    def gate_row(branch):
        return jnp.concatenate(
            [gt_ref[pl.ds((h * GROUP + g) * 3 + branch, 1), :] for g in range(GROUP)], axis=1)

    o_t = (gate_row(0) * o_cmp + gate_row(1) * (acc_sel / l_sel) + gate_row(2) * (acc_win / l_win))
    o_ref[...] = _unstack_output(o_t, tq).astype(o_ref.dtype)


def _nsa_attention(q, kvb, kvcc, gate, batch, seq, n_cmp, n_sel):
    t, qd = q.shape
    tq = min(seq, 256)
    nq = seq // tq
    ncp = kvcc.shape[3]
    nsp = max(8, -(-n_sel // 8) * 8)
    kern = functools.partial(_nsa_attn_kernel, n_cmp=n_cmp, n_sel=n_sel)
    return pl.pallas_call(
        kern,
        grid=(batch, NSA_KV_HEADS, nq),
        in_specs=[pl.BlockSpec((tq, SLAB), lambda b, h, i: (b * nq + i, h)),
                  pl.BlockSpec((seq, kvb.shape[1]), lambda b, h, i: (b, 0)),
                  pl.BlockSpec((1, 2, 1, ncp, HEAD_DIM), lambda b, h, i: (b, 0, h, 0, 0)),
                  pl.BlockSpec((tq, LANES), lambda b, h, i: (b * nq + i, 0))],
        out_specs=pl.BlockSpec((tq, SLAB), lambda b, h, i: (b * nq + i, h)),
        out_shape=jax.ShapeDtypeStruct((t, qd), BF16),
        scratch_shapes=[pltpu.VMEM((nq, tq, SLAB), BF16), pltpu.VMEM((nq, HEAD_DIM, tq), BF16),
                        pltpu.VMEM((nq, tq, SLAB), BF16), pltpu.VMEM((nq, HEAD_DIM, tq), BF16),
                        pltpu.VMEM((ncp, SLAB), BF16), pltpu.VMEM((HEAD_DIM, ncp), BF16),
                        pltpu.VMEM((nsp, tq), F32), pltpu.VMEM((LANES, tq), F32)],
        compiler_params=_params(("arbitrary", "arbitrary", "arbitrary")),
        name="nsa_attention",
    )(q, kvb, kvcc, gate)


def _swa_attn_kernel(sink_ref, q_ref, kvb_ref, o_ref, k_ref, v_ref, *, slabs_per_kv):
    hv = pl.program_id(1)
    qi = pl.program_id(2)
    tq = q_ref.shape[0]
    kd = SWA_KV_HEADS * HEAD_DIM

    @pl.when(qi == 0)
    def _():
        _build_kv_tiles(kvb_ref, 0, kd, kd, (hv // slabs_per_kv) * HEAD_DIM, k_ref, v_ref)

    s0 = qi * tq
    q_stack = _stack_queries(q_ref[...])
    q_pos = s0 + lax.broadcasted_iota(jnp.int32, (1, tq), 1)
    k_pos = s0 + lax.broadcasted_iota(jnp.int32, (tq, 1), 0)
    sink = jnp.concatenate([jnp.full((1, tq), sink_ref[hv * GROUP + g], F32) for g in range(GROUP)], axis=1)

    s = _dot_nt(k_ref[qi], q_stack) + _tile_lanes(_band_bias(q_pos, k_pos, SWA_WINDOW), GROUP)
    m = jnp.maximum(jnp.max(s, axis=0, keepdims=True), sink)
    p = jnp.exp(s - m)
    l = jnp.sum(p, axis=0, keepdims=True) + jnp.exp(sink - m)
    acc = _dot(v_ref[qi], p.astype(BF16))
    _, l, acc = _window_branch(q_stack, k_ref, v_ref, qi, tq, SWA_WINDOW, (m, l, acc))
    o_ref[...] = _unstack_output(acc / l, tq).astype(o_ref.dtype)


def _swa_attention(q, kvb, sinks, batch, seq):
    t, qd = q.shape
    tq = min(seq, 256)
    nq = seq // tq
    kern = functools.partial(_swa_attn_kernel, slabs_per_kv=qd // SLAB // SWA_KV_HEADS)
    return pl.pallas_call(
        kern,
        grid=(batch, qd // SLAB, nq),
        in_specs=[pl.BlockSpec(memory_space=pltpu.SMEM),
                  pl.BlockSpec((tq, SLAB), lambda b, h, i: (b * nq + i, h)),
                  pl.BlockSpec((seq, kvb.shape[1]), lambda b, h, i: (b, 0))],
        out_specs=pl.BlockSpec((tq, SLAB), lambda b, h, i: (b * nq + i, h)),
        out_shape=jax.ShapeDtypeStruct((t, qd), BF16),
        scratch_shapes=[pltpu.VMEM((nq, tq, SLAB), BF16), pltpu.VMEM((nq, HEAD_DIM, tq), BF16)],
        compiler_params=_params(("arbitrary", "arbitrary", "arbitrary")),
        name="swa_attention",
    )(sinks, q, kvb)


def _post_kernel(o_ref, x_ref, g_ref, gate1_ref, sh2_ref, sc2_ref, gate2_ref,
                 wo_ref, wup_ref, wdn_ref, out_ref):
    y = _dot(o_ref[...], wo_ref[...])
    x1 = x_ref[...] + (1.0 + gate1_ref[0]) * _rms(y, g_ref[1])
    hb = (_rms(x1, g_ref[2]) * (1.0 + sc2_ref[0]) + sh2_ref[0]).astype(BF16)
    d = x1.shape[1]
    acc = jnp.zeros_like(x1)
    for c in range(wup_ref.shape[1] // d):
        a = jnp.maximum(_dot(hb, wup_ref[:, c * d:(c + 1) * d]), 0.0)
        acc = acc + _dot((a * a).astype(BF16), wdn_ref[c * d:(c + 1) * d, :])
    out_ref[...] = x1 + (1.0 + gate2_ref[0]) * _rms(acc, g_ref[3])


def _post_mixer(o, x2, mod_rows, norm_g4, layer, batch, seq, wo, wup, wdn):
    t, d = x2.shape
    tm = min(seq, 512)
    nt = seq // tm
    row = lambda i: (i, 0)
    mod_row = lambda part: pl.BlockSpec((1, 1, d), lambda i: ((layer * batch + i // nt) * 6 + part, 0, 0))
    return pl.pallas_call(
        _post_kernel,
        grid=(t // tm,),
        in_specs=[pl.BlockSpec((tm, o.shape[1]), row),
                  pl.BlockSpec((tm, d), row),
                  pl.BlockSpec((4, 1, d), lambda i: (layer, 0, 0)),
                  mod_row(2), mod_row(3), mod_row(4), mod_row(5),
                  _resident(wo.shape, lambda i: (0, 0)),
                  _resident(wup.shape, lambda i: (0, 0)),
                  _resident(wdn.shape, lambda i: (0, 0))],
        out_specs=pl.BlockSpec((tm, d), row),
        out_shape=jax.ShapeDtypeStruct((t, d), F32),
        compiler_params=_params(("arbitrary",)),
        name="post_mixer_mlp",
    )(o, x2, norm_g4, mod_rows, mod_rows, mod_rows, mod_rows, wo, wup, wdn)


def kernel(x, c, positions, ada_w, ada_b, norm_g, nsa_w_in, nsa_w_out, nsa_cmp_pe, nsa_phi_w1, nsa_phi_b1,
           nsa_phi_w2, nsa_phi_b2, swa_w_in, swa_w_out, swa_sinks, mlp_w_up, mlp_w_down):
    batch, seq, d = x.shape
    depth = ada_w.shape[0]
    assert seq % min(seq, 256) == 0 and seq % SEL_BLOCK == 0, "sequence must tile into query blocks"
    t = batch * seq
    n_cmp = (seq - CMP_BLOCK) // CMP_STRIDE + 1
    n_sel = seq // SEL_BLOCK
    ncp = -(-(seq // CMP_STRIDE) // LANES) * LANES

    cos_t, sin_t = _rope_tables(positions)
    mod = _modulation(c, ada_w, ada_b)
    mod_rows = mod.reshape(depth * batch * 6, 1, d)
    norm_rows = norm_g.reshape(depth * 4, 1, d)
    norm_g4 = norm_g.reshape(depth * 4, 1, d)

    x2 = x.reshape(t, d)
    qd = d
    for i in range(depth):
        a = i // 2
        if i % 2 == 0:
            kd = NSA_KV_HEADS * HEAD_DIM
            w_in = nsa_w_in[a].astype(BF16)
            n_gate = w_in.shape[1] - qd - 6 * kd
            wg = jnp.pad(w_in[:, qd + 6 * kd:], ((0, 0), (0, LANES - n_gate)))
            q, kvc, kvb, gate = _pre_mixer("nsa", x2, mod_rows, norm_rows, i, batch, seq,
                                           (w_in[:, :qd], w_in[:, qd:qd + 6 * kd], wg), cos_t, sin_t)
            kvcc = _compress(kvc, batch, seq, nsa_cmp_pe[a], nsa_phi_w1[a], nsa_phi_b1[a],
                             nsa_phi_w2[a], nsa_phi_b2[a], ncp)
            o = _nsa_attention(q, kvb, kvcc, gate, batch, seq, n_cmp, n_sel)
            wo = nsa_w_out[a].astype(BF16)
        else:
            w_in = swa_w_in[a].astype(BF16)
            q, kvb = _pre_mixer("swa", x2, mod_rows, norm_rows, i, batch, seq,
                                (w_in[:, :qd], w_in[:, qd:]), cos_t, sin_t)
            o = _swa_attention(q, kvb, swa_sinks[a], batch, seq)
            wo = swa_w_out[a].astype(BF16)
        x2 = _post_mixer(o, x2, mod_rows, norm_g4, i, batch, seq, wo,
                         mlp_w_up[i].astype(BF16), mlp_w_down[i].astype(BF16))
    return x2.reshape(batch, seq, d)
```

```python
import functools

import jax
import jax.numpy as jnp
from jax import lax
from jax.experimental import pallas as pl
from jax.experimental.pallas import tpu as pltpu

F32 = jnp.float32
BF16 = jnp.bfloat16

HEAD_DIM = 64
ROPE_THETA = 10000.0
NORM_EPS = 1e-6
LOG2E = 1.4426950408889634
Q_SCALE = HEAD_DIM ** -0.5 * LOG2E
GROUP = 4
SLAB = GROUP * HEAD_DIM
LANES = 128
NSA_KV_HEADS = 4
NSA_WINDOW = 512
CMP_BLOCK = 32
CMP_STRIDE = 16
SEL_BLOCK = 64
SEL_TOPK = 8
SWA_KV_HEADS = 2
SWA_WINDOW = 128
VMEM_LIMIT = 56 * 1024 * 1024
NEG_INF = float("-inf")
MASKED = -1e30
ACC_ROWS = 80
NULL_LANE = LANES - 1
NSA_HEADS_PER_STEP = 2


def _dot(a, b):
    return jnp.dot(a, b, preferred_element_type=F32)


def _dot_nt(a, b):
    return lax.dot_general(a, b, (((1,), (1,)), ((), ())), preferred_element_type=F32)


def _params(sem):
    return pltpu.CompilerParams(dimension_semantics=sem, vmem_limit_bytes=VMEM_LIMIT)


def _resident(shape, index_map):
    return pl.BlockSpec(shape, index_map, pipeline_mode=pl.Buffered(1))


def _rope_table_kernel(pos_ref, inv_ref, cos_ref, sin_ref):
    ang = pos_ref[...].astype(F32) * inv_ref[...]
    lane = lax.broadcasted_iota(jnp.int32, ang.shape, 1)
    first_half = (lane % HEAD_DIM) < HEAD_DIM // 2
    cos_ref[...] = jnp.cos(ang)
    s = jnp.sin(ang)
    sin_ref[...] = jnp.where(first_half, -s, s)


def _rope_tables(positions):
    t = positions.size
    tm = min(t, 2048)
    inv = 1.0 / (ROPE_THETA ** (jnp.arange(0, HEAD_DIM, 2, dtype=F32) / HEAD_DIM))
    inv_lanes = jnp.tile(inv, LANES // (HEAD_DIM // 2)).reshape(1, LANES)
    return pl.pallas_call(
        _rope_table_kernel,
        grid=(t // tm,),
        in_specs=[pl.BlockSpec((tm, 1), lambda i: (i, 0)),
                  pl.BlockSpec((1, LANES), lambda i: (0, 0))],
        out_specs=[pl.BlockSpec((tm, LANES), lambda i: (i, 0)),
                   pl.BlockSpec((tm, LANES), lambda i: (i, 0))],
        out_shape=[jax.ShapeDtypeStruct((t, LANES), F32)] * 2,
        compiler_params=_params(("arbitrary",)),
        name="rope_tables",
    )(positions.reshape(t, 1), inv_lanes)


def _rope(v, cos, sin_signed):
    lane = lax.broadcasted_iota(jnp.int32, (v.shape[0], LANES), 1)
    first_half = (lane % HEAD_DIM) < HEAD_DIM // 2
    outs = []
    for j in range(v.shape[1] // LANES):
        s = v[:, j * LANES:(j + 1) * LANES]
        partner = jnp.where(first_half,
                            pltpu.roll(s, LANES - HEAD_DIM // 2, 1),
                            pltpu.roll(s, HEAD_DIM // 2, 1))
        outs.append(s * cos + partner * sin_signed)
    return outs[0] if len(outs) == 1 else jnp.concatenate(outs, axis=1)


def _mod_kernel(c_ref, w_ref, b_ref, o_ref):
    c = c_ref[...]
    cond = (c * jax.nn.sigmoid(c)).astype(BF16)
    o_ref[0] = _dot(cond, w_ref[0].astype(BF16)) + b_ref[0]


def _modulation(c, ada_w, ada_b):
    depth, d, n = ada_w.shape
    b = c.shape[0]
    tn = 1024
    return pl.pallas_call(
        _mod_kernel,
        grid=(depth, n // tn),
        in_specs=[pl.BlockSpec((b, d), lambda l, j: (0, 0)),
                  pl.BlockSpec((1, d, tn), lambda l, j: (l, 0, j)),
                  pl.BlockSpec((1, 1, tn), lambda l, j: (l, 0, j))],
        out_specs=pl.BlockSpec((1, b, tn), lambda l, j: (l, 0, j)),
        out_shape=jax.ShapeDtypeStruct((depth, b, n), F32),
        compiler_params=_params(("arbitrary", "arbitrary")),
        name="adaln_modulation",
    )(c, ada_w, ada_b.reshape(depth, 1, n))


def _rms(x, g):
    ms = jnp.mean(x * x, axis=-1, keepdims=True)
    return x * lax.rsqrt(ms + NORM_EPS) * g


def _nsa_pre_kernel(x_ref, g_ref, sh_ref, sc_ref, wq_ref, wkv_ref, wg_ref, cos_ref, sin_ref,
                    q_ref, kvc_ref, kvb_ref, gate_ref):
    h = _rms(x_ref[...], g_ref[0]) * (1.0 + sc_ref[0]) + sh_ref[0]
    hb = h.astype(BF16)
    cos = cos_ref[...]
    sin = sin_ref[...]
    q = _rope(_dot(hb, wq_ref[...]), cos, sin) * Q_SCALE
    q_ref[...] = q.astype(BF16)
    kv = _dot(hb, wkv_ref[...])
    kd = kvc_ref.shape[1] // 2
    kc = _rope(kv[:, 0:kd], cos, sin)
    kvc_ref[...] = jnp.concatenate([kc, kv[:, kd:2 * kd]], axis=1)
    ks = _rope(kv[:, 2 * kd:3 * kd], cos, sin)
    kw = _rope(kv[:, 4 * kd:5 * kd], cos, sin)
    kvb_ref[...] = jnp.concatenate(
        [ks, kv[:, 3 * kd:4 * kd], kw, kv[:, 5 * kd:6 * kd]], axis=1).astype(BF16)
    gate_ref[...] = jax.nn.sigmoid(_dot(hb, wg_ref[...]))


def _swa_pre_kernel(x_ref, g_ref, sh_ref, sc_ref, wq_ref, wkv_ref, cos_ref, sin_ref,
                    q_ref, kvb_ref):
    h = _rms(x_ref[...], g_ref[0]) * (1.0 + sc_ref[0]) + sh_ref[0]
    hb = h.astype(BF16)
    cos = cos_ref[...]
    sin = sin_ref[...]
    q = _rope(_dot(hb, wq_ref[...]), cos, sin) * Q_SCALE
    q_ref[...] = q.astype(BF16)
    kv = _dot(hb, wkv_ref[...])
    kd = kvb_ref.shape[1] // 2
    k = _rope(kv[:, 0:kd], cos, sin)
    kvb_ref[...] = jnp.concatenate([k, kv[:, kd:]], axis=1).astype(BF16)


def _pre_mixer(kind, x2, mod_rows, norm_rows, layer, batch, seq, weights, cos_t, sin_t):
    t, d = x2.shape
    tm = min(seq, 512)
    nt = seq // tm
    row = lambda b, i: (b * nt + i, 0)
    mod_row = lambda part: pl.BlockSpec((1, 1, d), lambda b, i: ((layer * batch + b) * 6 + part, 0, 0))
    common_in = [pl.BlockSpec((tm, d), row),
                 pl.BlockSpec((1, 1, d), lambda b, i: (layer * 4 + 0, 0, 0)),
                 mod_row(0), mod_row(1)]
    tab = [pl.BlockSpec((tm, LANES), row), pl.BlockSpec((tm, LANES), row)]
    if kind == "nsa":
        wq, wkv, wg = weights
        kd = NSA_KV_HEADS * HEAD_DIM
        return pl.pallas_call(
            _nsa_pre_kernel,
            grid=(batch, nt),
            in_specs=common_in + [_resident(wq.shape, lambda b, i: (0, 0)),
                                  _resident(wkv.shape, lambda b, i: (0, 0)),
                                  _resident(wg.shape, lambda b, i: (0, 0))] + tab,
            out_specs=[pl.BlockSpec((tm, wq.shape[1]), row),
                       pl.BlockSpec((tm, 2 * kd), row),
                       pl.BlockSpec((tm, 4 * kd), row),
                       pl.BlockSpec((tm, LANES), row)],
            out_shape=[jax.ShapeDtypeStruct((t, wq.shape[1]), BF16),
                       jax.ShapeDtypeStruct((t, 2 * kd), F32),
                       jax.ShapeDtypeStruct((t, 4 * kd), BF16),
                       jax.ShapeDtypeStruct((t, LANES), F32)],
            compiler_params=_params(("arbitrary", "arbitrary")),
            name="nsa_pre",
        )(x2, norm_rows, mod_rows, mod_rows, wq, wkv, wg, cos_t, sin_t)
    wq, wkv = weights
    kd = SWA_KV_HEADS * HEAD_DIM
    return pl.pallas_call(
        _swa_pre_kernel,
        grid=(batch, nt),
        in_specs=common_in + [_resident(wq.shape, lambda b, i: (0, 0)),
                              _resident(wkv.shape, lambda b, i: (0, 0))] + tab,
        out_specs=[pl.BlockSpec((tm, wq.shape[1]), row),
                   pl.BlockSpec((tm, 2 * kd), row)],
        out_shape=[jax.ShapeDtypeStruct((t, wq.shape[1]), BF16),
                   jax.ShapeDtypeStruct((t, 2 * kd), BF16)],
        compiler_params=_params(("arbitrary", "arbitrary")),
        name="swa_pre",
    )(x2, norm_rows, mod_rows, mod_rows, wq, wkv, cos_t, sin_t)


def _compress_kernel(x_ref, pe_ref, w1_ref, b1_ref, w2_ref, b2_ref, o_ref):
    nc = x_ref.shape[3]
    ncp = o_ref.shape[3]
    pe_lo = pe_ref[0, 0]
    pe_hi = pe_ref[0, 1]
    for h in range(x_ref.shape[2]):
        xh = x_ref[0, 0, h]
        lo = _dot((xh + pe_lo).astype(BF16), w1_ref[0, 0])
        hi = _dot((xh + pe_hi).astype(BF16), w1_ref[0, 1])
        pre = lo + pltpu.roll(hi, nc - 1, 0) + b1_ref[0]
        hid = jax.nn.gelu(pre, approximate=True).astype(BF16)
        out = _dot(hid, w2_ref[0]) + b2_ref[0]
        if ncp > nc:
            out = jnp.concatenate([out, jnp.zeros((ncp - nc, LANES), F32)], axis=0)
        o_ref[0, 0, h] = out


def _compress(kvc, batch, seq, pe, w1, b1, w2, b2, ncp):
    nc = seq // CMP_STRIDE
    hk = NSA_KV_HEADS
    chunk = CMP_STRIDE * HEAD_DIM
    xr = kvc.reshape(batch, nc, CMP_STRIDE, 2, hk, HEAD_DIM).transpose(0, 3, 4, 1, 2, 5)
    xr = xr.reshape(batch, 2, hk, nc, chunk)
    hidden = w1.shape[-1]
    pad = LANES - HEAD_DIM
    return pl.pallas_call(
        _compress_kernel,
        grid=(batch, 2),
        in_specs=[pl.BlockSpec((1, 1, hk, nc, chunk), lambda b, j: (b, j, 0, 0, 0)),
                  pl.BlockSpec((1, 2, 1, chunk), lambda b, j: (j, 0, 0, 0)),
                  pl.BlockSpec((1, 2, chunk, hidden), lambda b, j: (j, 0, 0, 0)),
                  pl.BlockSpec((1, 1, hidden), lambda b, j: (j, 0, 0)),
                  pl.BlockSpec((1, hidden, LANES), lambda b, j: (j, 0, 0)),
                  pl.BlockSpec((1, 1, LANES), lambda b, j: (j, 0, 0))],
        out_specs=pl.BlockSpec((1, 1, hk, ncp, LANES), lambda b, j: (b, j, 0, 0, 0)),
        out_shape=jax.ShapeDtypeStruct((batch, 2, hk, ncp, LANES), F32),
        compiler_params=_params(("arbitrary", "arbitrary")),
        name="nsa_compress",
    )(xr, pe.reshape(2, 2, 1, chunk), w1.astype(BF16).reshape(2, 2, chunk, hidden),
      b1.reshape(2, 1, hidden), jnp.pad(w2.astype(BF16), ((0, 0), (0, 0), (0, pad))),
      jnp.pad(b2.reshape(2, 1, HEAD_DIM), ((0, 0), (0, 0), (0, pad))))


def _tile_lanes(a, n):
    return jnp.concatenate([a] * n, axis=1)


def _lane(rows):
    return lax.broadcasted_iota(jnp.int32, (rows, LANES), 1)


def _query_heads(q_ref, slab):
    qf = q_ref[:, slab * SLAB:(slab + 1) * SLAB].astype(F32)
    heads = []
    for g in range(GROUP):
        x = qf[:, (g // 2) * LANES:(g // 2 + 1) * LANES]
        heads.append(pltpu.roll(x, HEAD_DIM, 1) if g % 2 else x)
    return heads


def _stack_queries(heads, extra=None):
    lane = _lane(heads[0].shape[0])
    tail = jnp.where(lane == NULL_LANE, MASKED, 0.0 if extra is None else extra)
    return jnp.concatenate([jnp.where(lane < HEAD_DIM, x, tail).astype(BF16) for x in heads], axis=0)


def _build_kv_tiles(k_src, v_src, col, odd, block_lanes, k_ref, vt_ref):
    nt, tq = k_ref.shape[0] - 1, k_ref.shape[1]
    lane = _lane(tq)
    row = lax.broadcasted_iota(jnp.int32, (tq, LANES), 0)
    ones_lane = jnp.where(lane == HEAD_DIM, 1.0, 0.0)
    for t in range(nt):
        kx = k_src[t * tq:(t + 1) * tq, col:col + LANES].astype(F32)
        if odd:
            kx = pltpu.roll(kx, HEAD_DIM, 1)
        if block_lanes:
            block = t * (tq // SEL_BLOCK) + row // SEL_BLOCK
            extra = jnp.where(lane == HEAD_DIM + block, 1.0, 0.0)
        else:
            extra = 0.0
        k_ref[t] = jnp.where(lane < HEAD_DIM, kx, extra).astype(BF16)
        vx = v_src[t * tq:(t + 1) * tq, col:col + LANES].astype(F32)
        if odd:
            vx = pltpu.roll(vx, HEAD_DIM, 1)
        vx = jnp.where(lane < HEAD_DIM, vx, ones_lane)
        vt_ref[t] = vx.T[0:ACC_ROWS].astype(BF16)
    k_ref[nt] = jnp.where(lane == NULL_LANE, 1.0, 0.0).astype(BF16)
    vt_ref[nt] = jnp.zeros(vt_ref.shape[1:], BF16)


def _unstack_output(o_t, tq):
    halves = []
    for p in range(GROUP // 2):
        pair = jnp.concatenate([o_t[:, (2 * p) * tq:(2 * p + 1) * tq],
                                o_t[:, (2 * p + 1) * tq:(2 * p + 2) * tq]], axis=0)
        halves.append(pair.T)
    return jnp.concatenate(halves, axis=1)


def _band_bias(q_pos, k_pos, window):
    d = q_pos - k_pos
    return _tile_lanes(jnp.where((d >= 0) & (d < window), 0.0, MASKED), GROUP)


def _first_tile(s, v_t, floor=None):
    m = jnp.max(s, axis=0, keepdims=True)
    if floor is not None:
        m = jnp.maximum(m, floor)
    return m, _dot(v_t, jnp.exp2(s - m).astype(BF16))


def _next_tile(m, acc, s, v_t):
    m_new = jnp.maximum(m, jnp.max(s, axis=0, keepdims=True))
    alpha = jnp.exp2(m - m_new)
    return m_new, alpha * acc + _dot(v_t, jnp.exp2(s - m_new).astype(BF16))


def _window_chain(q_stack, k_ref, vt_ref, qi, tq, window, floor=None):
    null_tile = k_ref.shape[0] - 1
    q_pos = qi * tq + lax.broadcasted_iota(jnp.int32, (1, tq), 1)
    k_pos = qi * tq + lax.broadcasted_iota(jnp.int32, (tq, 1), 0)
    s = _dot_nt(k_ref[qi], q_stack) + _band_bias(q_pos, k_pos, window)
    m, acc = _first_tile(s, vt_ref[qi], floor)
    keep = min(window, tq)
    for c in range(1, -(-window // tq) + 1):
        kt = jnp.where(qi >= c, qi - c, null_tile)
        s = _dot_nt(k_ref[kt, tq - keep:tq, :], q_stack)
        if c * tq + keep - 1 >= window:
            k_pos = (qi - c) * tq + (tq - keep) + lax.broadcasted_iota(jnp.int32, (keep, 1), 0)
            s = s + _band_bias(q_pos, k_pos, window)
        m, acc = _next_tile(m, acc, s, vt_ref[kt, :, tq - keep:tq])
    return m, acc


def _nsa_attn_kernel(q_ref, ksrc_ref, vsrc_ref, kwsrc_ref, vwsrc_ref, kcc_ref, vcc_ref, gate_ref, o_ref,
                     ks_ref, vs_ref, kw_ref, vw_ref, kc_ref, vc_ref, qs_ref, s_ref, p_ref, acc_ref, gt_ref,
                     *, n_cmp, n_sel, hps):
    hb = pl.program_id(1)
    qi = pl.program_id(2)
    tq = q_ref.shape[0]
    ncp = kc_ref.shape[1]
    nsp = -(-n_sel // 8) * 8

    @pl.when(qi == 0)
    def _():
        for hp in range(hps):
            col, odd = (hp // 2) * LANES, hp % 2 == 1
            _build_kv_tiles(ksrc_ref, vsrc_ref, col, odd, True, ks_ref.at[hp], vs_ref.at[hp])
            _build_kv_tiles(kwsrc_ref, vwsrc_ref, col, odd, False, kw_ref.at[hp], vw_ref.at[hp])
            kc_ref[hp] = kcc_ref[0, 0, hp].astype(BF16)
            vc_ref[hp] = vcc_ref[0, 0, hp].T[0:HEAD_DIM].astype(BF16)

    s0 = qi * tq
    q_pos = s0 + lax.broadcasted_iota(jnp.int32, (1, tq), 1)
    q_pos4 = _tile_lanes(q_pos, GROUP)
    k_pos = s0 + lax.broadcasted_iota(jnp.int32, (tq, 1), 0)
    causal = _tile_lanes(jnp.where(k_pos <= q_pos, 0.0, MASKED), GROUP)
    n_idx = lax.broadcasted_iota(jnp.int32, (ncp, 1), 0)
    valid = (n_idx * CMP_STRIDE + (CMP_BLOCK - 1) <= q_pos4) & (n_idx < n_cmp)
    j_blk = lax.broadcasted_iota(jnp.int32, (nsp, ncp), 0)
    n_blk = lax.broadcasted_iota(jnp.int32, (nsp, ncp), 1)
    overlap = (jnp.minimum(n_blk * CMP_STRIDE + CMP_BLOCK, j_blk * SEL_BLOCK + SEL_BLOCK)
               - jnp.maximum(n_blk * CMP_STRIDE, j_blk * SEL_BLOCK))
    overlap = jnp.where((n_blk < n_cmp) & (j_blk < n_sel), jnp.maximum(overlap, 0), 0)
    weight = (overlap.astype(F32) * (1.0 / CMP_STRIDE)).astype(BF16)
    j_row = lax.broadcasted_iota(jnp.int32, (nsp, tq), 0).astype(F32)
    cur = jnp.right_shift(s0 + lax.broadcasted_iota(jnp.int32, (nsp, tq), 1), 6).astype(F32)
    forced = (j_row == 0) | (j_row == cur) | (j_row == cur - 1)
    gt_ref[...] = gate_ref[...].T

    def sel_scores(hp, kt):
        s = _dot_nt(ks_ref[hp, jnp.maximum(kt, 0)], qs_ref[hp])
        s_ref[hp] = s
        return jnp.max(s, axis=0, keepdims=True)

    def front(hp):
        heads = _query_heads(q_ref, hp)
        q_plain = _stack_queries(heads)

        sc = jnp.where(valid, _dot_nt(kc_ref[hp], q_plain), NEG_INF)
        mc = jnp.max(sc, axis=0, keepdims=True)
        mc = jnp.where(mc == NEG_INF, 0.0, mc)
        ec = jnp.where(valid, jnp.exp2(sc - mc), 0.0)
        den = jnp.sum(ec, axis=0, keepdims=True)
        pc = (ec / jnp.where(den > 0, den, 1.0)).astype(BF16)
        o_cmp = _dot(vc_ref[hp], pc)

        imp4 = _dot(weight, pc)
        imp = imp4[:, 0:tq]
        for g in range(1, GROUP):
            imp = imp + imp4[:, g * tq:(g + 1) * tq]
        score = jnp.where(forced, jnp.inf, jnp.where(j_row <= cur, imp, NEG_INF))
        chosen = jnp.zeros((nsp, tq), F32)
        for _ in range(min(SEL_TOPK, n_sel)):
            best = jnp.max(score, axis=0, keepdims=True)
            idx = jnp.min(jnp.where(score == best, j_row, float(nsp)), axis=0, keepdims=True)
            pick = j_row == idx
            chosen = jnp.where(pick, 1.0, chosen)
            score = jnp.where(pick, NEG_INF, score)

        pieces = [jnp.zeros((HEAD_DIM, tq), F32), jnp.where(chosen > 0.5, 0.0, MASKED)]
        if LANES - HEAD_DIM - nsp > 0:
            pieces.append(jnp.zeros((LANES - HEAD_DIM - nsp, tq), F32))
        qs_ref[hp] = _stack_queries(heads, jnp.concatenate(pieces, axis=0).T)

        _, acc_win = _window_chain(q_plain, kw_ref.at[hp], vw_ref.at[hp], qi, tq, NSA_WINDOW)

        s = _dot_nt(ks_ref[hp, qi], qs_ref[hp]) + causal
        m = jnp.max(s, axis=0, keepdims=True)
        p_ref[hp] = jnp.exp2(s - m).astype(BF16)
        acc_ref[hp] = jnp.zeros((ACC_ROWS, GROUP * tq), F32)
        return o_cmp, acc_win, (m, jnp.ones_like(m), sel_scores(hp, qi - 1))

    fronts = [front(hp) for hp in range(hps)]

    def sel_body(j, carry):
        out = ()
        for hp in range(hps):
            m, alpha, s_max = carry[3 * hp:3 * hp + 3]
            acc_ref[hp] = alpha * acc_ref[hp] + _dot(vs_ref[hp, qi - j], p_ref[hp])
            m_new = jnp.maximum(m, s_max)
            p_ref[hp] = jnp.exp2(s_ref[hp] - m_new).astype(BF16)
            out += (m_new, jnp.exp2(m - m_new), sel_scores(hp, qi - 2 - j))
        return out

    final = lax.fori_loop(0, qi, sel_body, tuple(x for f in fronts for x in f[2]))

    def normalised(a):
        return a[0:HEAD_DIM] / a[HEAD_DIM:HEAD_DIM + 1]

    for hp in range(hps):
        o_cmp, acc_win, _ = fronts[hp]
        alpha = final[3 * hp + 1]
        acc_sel = alpha * acc_ref[hp] + _dot(vs_ref[hp, 0], p_ref[hp])

        def gate_row(branch):
            first = ((hb * hps + hp) * GROUP) * 3 + branch
            return jnp.concatenate([gt_ref[pl.ds(first + 3 * g, 1), :] for g in range(GROUP)], axis=1)

        o_t = gate_row(0) * o_cmp + gate_row(1) * normalised(acc_sel) + gate_row(2) * normalised(acc_win)
        o_ref[:, hp * SLAB:(hp + 1) * SLAB] = _unstack_output(o_t, tq).astype(o_ref.dtype)


def _nsa_attention(q, kvb, kvcc, gate, batch, seq, n_cmp, n_sel):
    t, qd = q.shape
    tq = min(seq, 256)
    nq = seq // tq
    ncp = kvcc.shape[3]
    hps = NSA_HEADS_PER_STEP
    assert HEAD_DIM + n_sel <= NULL_LANE and tq % SEL_BLOCK == 0 and hps % 2 == 0
    kern = functools.partial(_nsa_attn_kernel, n_cmp=n_cmp, n_sel=n_sel, hps=hps)
    kv_lanes = hps // 2 * LANES
    blocks = NSA_KV_HEADS * HEAD_DIM // kv_lanes

    def kv_block(which):
        return pl.BlockSpec((seq, kv_lanes), lambda b, h, i: (b, which * blocks + h))

    def per_head(shape):
        return pltpu.VMEM((hps,) + shape, BF16)

    return pl.pallas_call(
        kern,
        grid=(batch, NSA_KV_HEADS // hps, nq),
        in_specs=[pl.BlockSpec((tq, hps * SLAB), lambda b, h, i: (b * nq + i, h)),
                  kv_block(0), kv_block(1), kv_block(2), kv_block(3),
                  pl.BlockSpec((1, 1, hps, ncp, LANES), lambda b, h, i: (b, 0, h, 0, 0)),
                  pl.BlockSpec((1, 1, hps, ncp, LANES), lambda b, h, i: (b, 1, h, 0, 0)),
                  pl.BlockSpec((tq, LANES), lambda b, h, i: (b * nq + i, 0))],
        out_specs=pl.BlockSpec((tq, hps * SLAB), lambda b, h, i: (b * nq + i, h)),
        out_shape=jax.ShapeDtypeStruct((t, qd), BF16),
        scratch_shapes=[per_head((nq + 1, tq, LANES)), per_head((nq + 1, ACC_ROWS, tq)),
                        per_head((nq + 1, tq, LANES)), per_head((nq + 1, ACC_ROWS, tq)),
                        per_head((ncp, LANES)), per_head((HEAD_DIM, ncp)),
                        per_head((GROUP * tq, LANES)),
                        pltpu.VMEM((hps, tq, GROUP * tq), F32), per_head((tq, GROUP * tq)),
                        pltpu.VMEM((hps, ACC_ROWS, GROUP * tq), F32),
                        pltpu.VMEM((LANES, tq), F32)],
        compiler_params=_params(("arbitrary", "arbitrary", "arbitrary")),
        name="nsa_attention",
    )(q, kvb, kvb, kvb, kvb, kvcc, kvcc, gate)


def _swa_attn_kernel(sink_ref, q_ref, ksrc_ref, vsrc_ref, o_ref, k_ref, v_ref):
    qi = pl.program_id(1)
    tq = q_ref.shape[0]
    n_slabs = q_ref.shape[1] // SLAB
    slabs_per_kv = n_slabs // SWA_KV_HEADS

    @pl.when(qi == 0)
    def _():
        for kvh in range(SWA_KV_HEADS):
            _build_kv_tiles(ksrc_ref, vsrc_ref, 0, kvh == 1, False, k_ref.at[kvh], v_ref.at[kvh])

    for hv in range(n_slabs):
        kvh = hv // slabs_per_kv
        q_stack = _stack_queries(_query_heads(q_ref, hv))
        sink = jnp.concatenate(
            [jnp.full((1, tq), sink_ref[hv * GROUP + g] * LOG2E, F32) for g in range(GROUP)], axis=1)
        m, acc = _window_chain(q_stack, k_ref.at[kvh], v_ref.at[kvh], qi, tq, SWA_WINDOW, floor=sink)
        denom = acc[HEAD_DIM:HEAD_DIM + 1] + jnp.exp2(sink - m)
        o_ref[:, hv * SLAB:(hv + 1) * SLAB] = _unstack_output(acc[0:HEAD_DIM] / denom, tq).astype(o_ref.dtype)


def _swa_attention(q, kvb, sinks, batch, seq):
    t, qd = q.shape
    tq = min(seq, 256)
    nq = seq // tq
    assert SWA_KV_HEADS * HEAD_DIM == LANES and SWA_KV_HEADS == 2
    return pl.pallas_call(
        _swa_attn_kernel,
        grid=(batch, nq),
        in_specs=[pl.BlockSpec(memory_space=pltpu.SMEM),
                  pl.BlockSpec((tq, qd), lambda b, i: (b * nq + i, 0)),
                  pl.BlockSpec((seq, LANES), lambda b, i: (b, 0)),
                  pl.BlockSpec((seq, LANES), lambda b, i: (b, 1))],
        out_specs=pl.BlockSpec((tq, qd), lambda b, i: (b * nq + i, 0)),
        out_shape=jax.ShapeDtypeStruct((t, qd), BF16),
        scratch_shapes=[pltpu.VMEM((SWA_KV_HEADS, nq + 1, tq, LANES), BF16),
                        pltpu.VMEM((SWA_KV_HEADS, nq + 1, ACC_ROWS, tq), BF16)],
        compiler_params=_params(("arbitrary", "arbitrary")),
        name="swa_attention",
    )(sinks, q, kvb, kvb)


def _post_kernel(o_ref, x_ref, g_ref, gate1_ref, sh2_ref, sc2_ref, gate2_ref,
                 wo_ref, wup_ref, wdn_ref, out_ref):
    y = _dot(o_ref[...], wo_ref[...])
    x1 = x_ref[...] + (1.0 + gate1_ref[0]) * _rms(y, g_ref[1])
    hb = (_rms(x1, g_ref[2]) * (1.0 + sc2_ref[0]) + sh2_ref[0]).astype(BF16)
    d = x1.shape[1]
    acc = jnp.zeros_like(x1)
    for c in range(wup_ref.shape[1] // d):
        a = jnp.maximum(_dot(hb, wup_ref[:, c * d:(c + 1) * d]), 0.0)
        acc = acc + _dot((a * a).astype(BF16), wdn_ref[c * d:(c + 1) * d, :])
    out_ref[...] = x1 + (1.0 + gate2_ref[0]) * _rms(acc, g_ref[3])


def _post_mixer(o, x2, mod_rows, norm_g4, layer, batch, seq, wo, wup, wdn):
    t, d = x2.shape
    tm = min(seq, 512)
    nt = seq // tm
    row = lambda i: (i, 0)
    mod_row = lambda part: pl.BlockSpec((1, 1, d), lambda i: ((layer * batch + i // nt) * 6 + part, 0, 0))
    return pl.pallas_call(
        _post_kernel,
        grid=(t // tm,),
        in_specs=[pl.BlockSpec((tm, o.shape[1]), row),
                  pl.BlockSpec((tm, d), row),
                  pl.BlockSpec((4, 1, d), lambda i: (layer, 0, 0)),
                  mod_row(2), mod_row(3), mod_row(4), mod_row(5),
                  _resident(wo.shape, lambda i: (0, 0)),
                  _resident(wup.shape, lambda i: (0, 0)),
                  _resident(wdn.shape, lambda i: (0, 0))],
        out_specs=pl.BlockSpec((tm, d), row),
        out_shape=jax.ShapeDtypeStruct((t, d), F32),
        compiler_params=_params(("arbitrary",)),
        name="post_mixer_mlp",
    )(o, x2, norm_g4, mod_rows, mod_rows, mod_rows, mod_rows, wo, wup, wdn)


def kernel(x, c, positions, ada_w, ada_b, norm_g, nsa_w_in, nsa_w_out, nsa_cmp_pe, nsa_phi_w1, nsa_phi_b1,
           nsa_phi_w2, nsa_phi_b2, swa_w_in, swa_w_out, swa_sinks, mlp_w_up, mlp_w_down):
    batch, seq, d = x.shape
    depth = ada_w.shape[0]
    assert seq % min(seq, 256) == 0 and seq % SEL_BLOCK == 0, "sequence must tile into query blocks"
    t = batch * seq
    n_cmp = (seq - CMP_BLOCK) // CMP_STRIDE + 1
    n_sel = seq // SEL_BLOCK
    ncp = -(-(seq // CMP_STRIDE) // LANES) * LANES

    cos_t, sin_t = _rope_tables(positions)
    mod = _modulation(c, ada_w, ada_b)
    mod_rows = mod.reshape(depth * batch * 6, 1, d)
    norm_rows = norm_g.reshape(depth * 4, 1, d)

    x2 = x.reshape(t, d)
    qd = d
    for i in range(depth):
        a = i // 2
        if i % 2 == 0:
            kd = NSA_KV_HEADS * HEAD_DIM
            w_in = nsa_w_in[a].astype(BF16)
            n_gate = w_in.shape[1] - qd - 6 * kd
            wg = jnp.pad(w_in[:, qd + 6 * kd:], ((0, 0), (0, LANES - n_gate)))
            q, kvc, kvb, gate = _pre_mixer("nsa", x2, mod_rows, norm_rows, i, batch, seq,
                                           (w_in[:, :qd], w_in[:, qd:qd + 6 * kd], wg), cos_t, sin_t)
            kvcc = _compress(kvc, batch, seq, nsa_cmp_pe[a], nsa_phi_w1[a], nsa_phi_b1[a],
                             nsa_phi_w2[a], nsa_phi_b2[a], ncp)
            o = _nsa_attention(q, kvb, kvcc, gate, batch, seq, n_cmp, n_sel)
            wo = nsa_w_out[a].astype(BF16)
        else:
            w_in = swa_w_in[a].astype(BF16)
            q, kvb = _pre_mixer("swa", x2, mod_rows, norm_rows, i, batch, seq,
                                (w_in[:, :qd], w_in[:, qd:]), cos_t, sin_t)
            o = _swa_attention(q, kvb, swa_sinks[a], batch, seq)
            wo = swa_w_out[a].astype(BF16)
        x2 = _post_mixer(o, x2, mod_rows, norm_rows, i, batch, seq, wo,
                         mlp_w_up[i].astype(BF16), mlp_w_down[i].astype(BF16))
    return x2.reshape(batch, seq, d)
```

```python
import functools

import jax
import jax.numpy as jnp
from jax import lax
from jax.experimental import pallas as pl
from jax.experimental.pallas import tpu as pltpu

F32 = jnp.float32
BF16 = jnp.bfloat16

HEAD_DIM = 64
ROPE_THETA = 10000.0
NORM_EPS = 1e-6
LOG2E = 1.4426950408889634
Q_SCALE = HEAD_DIM ** -0.5 * LOG2E
GROUP = 4
SLAB = GROUP * HEAD_DIM
LANES = 128
NSA_KV_HEADS = 4
NSA_WINDOW = 512
CMP_BLOCK = 32
CMP_STRIDE = 16
SEL_BLOCK = 64
SEL_TOPK = 8
SWA_KV_HEADS = 2
SWA_WINDOW = 128
VMEM_LIMIT = 56 * 1024 * 1024
NEG_INF = float("-inf")
MASKED = -1e30
ACC_ROWS = 80
NULL_LANE = LANES - 1
NSA_HEADS_PER_STEP = 4


def _dot(a, b):
    return jnp.dot(a, b, preferred_element_type=F32)


def _dot_nt(a, b):
    return lax.dot_general(a, b, (((1,), (1,)), ((), ())), preferred_element_type=F32)


def _params(sem):
    return pltpu.CompilerParams(dimension_semantics=sem, vmem_limit_bytes=VMEM_LIMIT)


def _resident(shape, index_map):
    return pl.BlockSpec(shape, index_map, pipeline_mode=pl.Buffered(1))


def _rope_table_kernel(pos_ref, inv_ref, cos_ref, sin_ref):
    ang = pos_ref[...].astype(F32) * inv_ref[...]
    lane = lax.broadcasted_iota(jnp.int32, ang.shape, 1)
    first_half = (lane % HEAD_DIM) < HEAD_DIM // 2
    cos_ref[...] = jnp.cos(ang)
    s = jnp.sin(ang)
    sin_ref[...] = jnp.where(first_half, -s, s)


def _rope_tables(positions):
    t = positions.size
    tm = min(t, 2048)
    inv = 1.0 / (ROPE_THETA ** (jnp.arange(0, HEAD_DIM, 2, dtype=F32) / HEAD_DIM))
    inv_lanes = jnp.tile(inv, LANES // (HEAD_DIM // 2)).reshape(1, LANES)
    return pl.pallas_call(
        _rope_table_kernel,
        grid=(t // tm,),
        in_specs=[pl.BlockSpec((tm, 1), lambda i: (i, 0)),
                  pl.BlockSpec((1, LANES), lambda i: (0, 0))],
        out_specs=[pl.BlockSpec((tm, LANES), lambda i: (i, 0)),
                   pl.BlockSpec((tm, LANES), lambda i: (i, 0))],
        out_shape=[jax.ShapeDtypeStruct((t, LANES), F32)] * 2,
        compiler_params=_params(("arbitrary",)),
        name="rope_tables",
    )(positions.reshape(t, 1), inv_lanes)


def _rope(v, cos, sin_signed):
    lane = lax.broadcasted_iota(jnp.int32, (v.shape[0], LANES), 1)
    first_half = (lane % HEAD_DIM) < HEAD_DIM // 2
    outs = []
    for j in range(v.shape[1] // LANES):
        s = v[:, j * LANES:(j + 1) * LANES]
        partner = jnp.where(first_half,
                            pltpu.roll(s, LANES - HEAD_DIM // 2, 1),
                            pltpu.roll(s, HEAD_DIM // 2, 1))
        outs.append(s * cos + partner * sin_signed)
    return outs[0] if len(outs) == 1 else jnp.concatenate(outs, axis=1)


def _mod_kernel(c_ref, w_ref, b_ref, o_ref):
    c = c_ref[...]
    cond = (c * jax.nn.sigmoid(c)).astype(BF16)
    o_ref[0] = _dot(cond, w_ref[0].astype(BF16)) + b_ref[0]


def _modulation(c, ada_w, ada_b):
    depth, d, n = ada_w.shape
    b = c.shape[0]
    tn = 1024
    return pl.pallas_call(
        _mod_kernel,
        grid=(depth, n // tn),
        in_specs=[pl.BlockSpec((b, d), lambda l, j: (0, 0)),
                  pl.BlockSpec((1, d, tn), lambda l, j: (l, 0, j)),
                  pl.BlockSpec((1, 1, tn), lambda l, j: (l, 0, j))],
        out_specs=pl.BlockSpec((1, b, tn), lambda l, j: (l, 0, j)),
        out_shape=jax.ShapeDtypeStruct((depth, b, n), F32),
        compiler_params=_params(("arbitrary", "arbitrary")),
        name="adaln_modulation",
    )(c, ada_w, ada_b.reshape(depth, 1, n))


def _rms(x, g):
    ms = jnp.mean(x * x, axis=-1, keepdims=True)
    return x * lax.rsqrt(ms + NORM_EPS) * g


def _nsa_pre_kernel(x_ref, g_ref, sh_ref, sc_ref, wq_ref, wkv_ref, wg_ref, cos_ref, sin_ref,
                    q_ref, kvc_ref, kvb_ref, gate_ref):
    h = _rms(x_ref[...], g_ref[0]) * (1.0 + sc_ref[0]) + sh_ref[0]
    hb = h.astype(BF16)
    cos = cos_ref[...]
    sin = sin_ref[...]
    q = _rope(_dot(hb, wq_ref[...]), cos, sin) * Q_SCALE
    q_ref[...] = q.astype(BF16)
    kv = _dot(hb, wkv_ref[...])
    kd = kvc_ref.shape[1] // 2
    kc = _rope(kv[:, 0:kd], cos, sin)
    kvc_ref[...] = jnp.concatenate([kc, kv[:, kd:2 * kd]], axis=1)
    ks = _rope(kv[:, 2 * kd:3 * kd], cos, sin)
    kw = _rope(kv[:, 4 * kd:5 * kd], cos, sin)
    kvb_ref[...] = jnp.concatenate(
        [ks, kv[:, 3 * kd:4 * kd], kw, kv[:, 5 * kd:6 * kd]], axis=1).astype(BF16)
    gate_ref[...] = jax.nn.sigmoid(_dot(hb, wg_ref[...]))


def _swa_pre_kernel(x_ref, g_ref, sh_ref, sc_ref, wq_ref, wkv_ref, cos_ref, sin_ref,
                    q_ref, kvb_ref):
    h = _rms(x_ref[...], g_ref[0]) * (1.0 + sc_ref[0]) + sh_ref[0]
    hb = h.astype(BF16)
    cos = cos_ref[...]
    sin = sin_ref[...]
    q = _rope(_dot(hb, wq_ref[...]), cos, sin) * Q_SCALE
    q_ref[...] = q.astype(BF16)
    kv = _dot(hb, wkv_ref[...])
    kd = kvb_ref.shape[1] // 2
    k = _rope(kv[:, 0:kd], cos, sin)
    kvb_ref[...] = jnp.concatenate([k, kv[:, kd:]], axis=1).astype(BF16)


def _pre_mixer(kind, x2, mod_rows, norm_rows, layer, batch, seq, weights, cos_t, sin_t):
    t, d = x2.shape
    tm = min(seq, 512)
    nt = seq // tm
    row = lambda b, i: (b * nt + i, 0)
    mod_row = lambda part: pl.BlockSpec((1, 1, d), lambda b, i: ((layer * batch + b) * 6 + part, 0, 0))
    common_in = [pl.BlockSpec((tm, d), row),
                 pl.BlockSpec((1, 1, d), lambda b, i: (layer * 4 + 0, 0, 0)),
                 mod_row(0), mod_row(1)]
    tab = [pl.BlockSpec((tm, LANES), row), pl.BlockSpec((tm, LANES), row)]
    if kind == "nsa":
        wq, wkv, wg = weights
        kd = NSA_KV_HEADS * HEAD_DIM
        return pl.pallas_call(
            _nsa_pre_kernel,
            grid=(batch, nt),
            in_specs=common_in + [_resident(wq.shape, lambda b, i: (0, 0)),
                                  _resident(wkv.shape, lambda b, i: (0, 0)),
                                  _resident(wg.shape, lambda b, i: (0, 0))] + tab,
            out_specs=[pl.BlockSpec((tm, wq.shape[1]), row),
                       pl.BlockSpec((tm, 2 * kd), row),
                       pl.BlockSpec((tm, 4 * kd), row),
                       pl.BlockSpec((tm, LANES), row)],
            out_shape=[jax.ShapeDtypeStruct((t, wq.shape[1]), BF16),
                       jax.ShapeDtypeStruct((t, 2 * kd), F32),
                       jax.ShapeDtypeStruct((t, 4 * kd), BF16),
                       jax.ShapeDtypeStruct((t, LANES), F32)],
            compiler_params=_params(("arbitrary", "arbitrary")),
            name="nsa_pre",
        )(x2, norm_rows, mod_rows, mod_rows, wq, wkv, wg, cos_t, sin_t)
    wq, wkv = weights
    kd = SWA_KV_HEADS * HEAD_DIM
    return pl.pallas_call(
        _swa_pre_kernel,
        grid=(batch, nt),
        in_specs=common_in + [_resident(wq.shape, lambda b, i: (0, 0)),
                              _resident(wkv.shape, lambda b, i: (0, 0))] + tab,
        out_specs=[pl.BlockSpec((tm, wq.shape[1]), row),
                   pl.BlockSpec((tm, 2 * kd), row)],
        out_shape=[jax.ShapeDtypeStruct((t, wq.shape[1]), BF16),
                   jax.ShapeDtypeStruct((t, 2 * kd), BF16)],
        compiler_params=_params(("arbitrary", "arbitrary")),
        name="swa_pre",
    )(x2, norm_rows, mod_rows, mod_rows, wq, wkv, cos_t, sin_t)


def _compress_kernel(x_ref, pe_ref, w1_ref, b1_ref, w2_ref, b2_ref, o_ref):
    nc = x_ref.shape[0] // CMP_STRIDE
    ncp = o_ref.shape[3]

    def chunk_rows(half):
        return jnp.concatenate(
            [x_ref[pl.ds(l, nc, stride=CMP_STRIDE), :] + pe_ref[0, CMP_STRIDE * half + l]
             for l in range(CMP_STRIDE)], axis=1).astype(BF16)

    lo = _dot(chunk_rows(0), w1_ref[0, 0])
    hi = _dot(chunk_rows(1), w1_ref[0, 1])
    pre = lo + pltpu.roll(hi, nc - 1, 0) + b1_ref[0]
    hid = jax.nn.gelu(pre, approximate=True).astype(BF16)
    out = _dot(hid, w2_ref[0]) + b2_ref[0]
    if ncp > nc:
        out = jnp.concatenate([out, jnp.zeros((ncp - nc, 2 * LANES), F32)], axis=0)
    o_ref[0, 0, 0] = out[:, 0:LANES]
    o_ref[0, 0, 1] = out[:, LANES:2 * LANES]


def _two_heads(w):
    eye = jnp.eye(2, dtype=w.dtype)
    blk = w[..., None, :, None, :] * eye[:, None, :, None]
    return blk.reshape(w.shape[:-2] + (2 * w.shape[-2], 2 * w.shape[-1]))


def _compress(kvc, batch, seq, pe, w1, b1, w2, b2, ncp):
    hk = NSA_KV_HEADS
    slabs = hk * HEAD_DIM // LANES
    hidden = w1.shape[-1]
    pad = LANES - HEAD_DIM
    half = CMP_BLOCK // 2
    w1p = _two_heads(w1.astype(BF16).reshape(2, 2, half, HEAD_DIM, hidden))
    w1p = w1p.reshape(2, 2, half * LANES, 2 * hidden)
    w2p = _two_heads(jnp.pad(w2.astype(BF16), ((0, 0), (0, 0), (0, pad))))
    b2p = jnp.pad(b2, ((0, 0), (0, pad)))
    return pl.pallas_call(
        _compress_kernel,
        grid=(batch, 2, slabs),
        in_specs=[pl.BlockSpec((seq, LANES), lambda b, j, s: (b, j * slabs + s)),
                  pl.BlockSpec((1, CMP_BLOCK, 1, LANES), lambda b, j, s: (j, 0, 0, 0)),
                  pl.BlockSpec((1, 2, half * LANES, 2 * hidden), lambda b, j, s: (j, 0, 0, 0)),
                  pl.BlockSpec((1, 1, 2 * hidden), lambda b, j, s: (j, 0, 0)),
                  pl.BlockSpec((1, 2 * hidden, 2 * LANES), lambda b, j, s: (j, 0, 0)),
                  pl.BlockSpec((1, 1, 2 * LANES), lambda b, j, s: (j, 0, 0))],
        out_specs=pl.BlockSpec((1, 1, 2, ncp, LANES), lambda b, j, s: (b, j, s, 0, 0)),
        out_shape=jax.ShapeDtypeStruct((batch, 2, hk, ncp, LANES), F32),
        compiler_params=_params(("arbitrary", "arbitrary", "arbitrary")),
        name="nsa_compress",
    )(kvc, jnp.tile(pe, (1, 1, 2)).reshape(2, CMP_BLOCK, 1, LANES), w1p,
      jnp.tile(b1, (1, 2)).reshape(2, 1, 2 * hidden), w2p, jnp.tile(b2p, (1, 2)).reshape(2, 1, 2 * LANES))


def _tile_lanes(a, n):
    return jnp.concatenate([a] * n, axis=1)


def _lane(rows):
    return lax.broadcasted_iota(jnp.int32, (rows, LANES), 1)


def _query_heads(q_ref, slab):
    qf = q_ref[:, slab * SLAB:(slab + 1) * SLAB].astype(F32)
    heads = []
    for g in range(GROUP):
        x = qf[:, (g // 2) * LANES:(g // 2 + 1) * LANES]
        heads.append(pltpu.roll(x, HEAD_DIM, 1) if g % 2 else x)
    return heads


def _stack_queries(heads, extra=None):
    lane = _lane(heads[0].shape[0])
    tail = jnp.where(lane == NULL_LANE, MASKED, 0.0 if extra is None else extra)
    return jnp.concatenate([jnp.where(lane < HEAD_DIM, x, tail).astype(BF16) for x in heads], axis=0)


def _build_kv_tiles(k_src, v_src, col, odd, block_lanes, k_ref, vt_ref):
    nt, tq = k_ref.shape[0] - 1, k_ref.shape[1]
    lane = _lane(tq)
    row = lax.broadcasted_iota(jnp.int32, (tq, LANES), 0)
    ones_lane = jnp.where(lane == HEAD_DIM, 1.0, 0.0)
    for t in range(nt):
        kx = k_src[t * tq:(t + 1) * tq, col:col + LANES].astype(F32)
        if odd:
            kx = pltpu.roll(kx, HEAD_DIM, 1)
        if block_lanes:
            block = t * (tq // SEL_BLOCK) + row // SEL_BLOCK
            extra = jnp.where(lane == HEAD_DIM + block, 1.0, 0.0)
        else:
            extra = 0.0
        k_ref[t] = jnp.where(lane < HEAD_DIM, kx, extra).astype(BF16)
        vx = v_src[t * tq:(t + 1) * tq, col:col + LANES].astype(F32)
        if odd:
            vx = pltpu.roll(vx, HEAD_DIM, 1)
        vx = jnp.where(lane < HEAD_DIM, vx, ones_lane)
        vt_ref[t] = vx.T[0:ACC_ROWS].astype(BF16)
    k_ref[nt] = jnp.where(lane == NULL_LANE, 1.0, 0.0).astype(BF16)
    vt_ref[nt] = jnp.zeros(vt_ref.shape[1:], BF16)


def _unstack_output(o_t, tq):
    halves = []
    for p in range(GROUP // 2):
        pair = jnp.concatenate([o_t[:, (2 * p) * tq:(2 * p + 1) * tq],
                                o_t[:, (2 * p + 1) * tq:(2 * p + 2) * tq]], axis=0)
        halves.append(pair.T)
    return jnp.concatenate(halves, axis=1)


def _band_bias(q_pos, k_pos, window):
    d = q_pos - k_pos
    return _tile_lanes(jnp.where((d >= 0) & (d < window), 0.0, MASKED), GROUP)


def _first_tile(s, v_t, floor=None):
    m = jnp.max(s, axis=0, keepdims=True)
    if floor is not None:
        m = jnp.maximum(m, floor)
    return m, _dot(v_t, jnp.exp2(s - m).astype(BF16))


def _next_tile(m, acc, s, v_t):
    m_new = jnp.maximum(m, jnp.max(s, axis=0, keepdims=True))
    alpha = jnp.exp2(m - m_new)
    return m_new, alpha * acc + _dot(v_t, jnp.exp2(s - m_new).astype(BF16))


def _window_chain(q_stack, k_ref, vt_ref, qi, tq, window, floor=None):
    null_tile = k_ref.shape[0] - 1
    q_pos = qi * tq + lax.broadcasted_iota(jnp.int32, (1, tq), 1)
    k_pos = qi * tq + lax.broadcasted_iota(jnp.int32, (tq, 1), 0)
    s = _dot_nt(k_ref[qi], q_stack) + _band_bias(q_pos, k_pos, window)
    m, acc = _first_tile(s, vt_ref[qi], floor)
    keep = min(window, tq)
    for c in range(1, -(-window // tq) + 1):
        kt = jnp.where(qi >= c, qi - c, null_tile)
        s = _dot_nt(k_ref[kt, tq - keep:tq, :], q_stack)
        if c * tq + keep - 1 >= window:
            k_pos = (qi - c) * tq + (tq - keep) + lax.broadcasted_iota(jnp.int32, (keep, 1), 0)
            s = s + _band_bias(q_pos, k_pos, window)
        m, acc = _next_tile(m, acc, s, vt_ref[kt, :, tq - keep:tq])
    return m, acc


def _nsa_attn_kernel(q_ref, ksrc_ref, vsrc_ref, kwsrc_ref, vwsrc_ref, kcc_ref, vcc_ref, gate_ref, o_ref,
                     ks_ref, vs_ref, kw_ref, vw_ref, kc_ref, vc_ref, qs_ref, s_ref, p_ref, acc_ref, gt_ref,
                     *, n_cmp, n_sel, hps):
    hb = pl.program_id(1)
    qi = pl.program_id(2)
    tq = q_ref.shape[0]
    ncp = kc_ref.shape[1]
    nsp = -(-n_sel // 8) * 8

    @pl.when(qi == 0)
    def _():
        for hp in range(hps):
            col, odd = (hp // 2) * LANES, hp % 2 == 1
            _build_kv_tiles(ksrc_ref, vsrc_ref, col, odd, True, ks_ref.at[hp], vs_ref.at[hp])
            _build_kv_tiles(kwsrc_ref, vwsrc_ref, col, odd, False, kw_ref.at[hp], vw_ref.at[hp])
            kc_ref[hp] = kcc_ref[0, 0, hp].astype(BF16)
            vc_ref[hp] = vcc_ref[0, 0, hp].T[0:HEAD_DIM].astype(BF16)

    s0 = qi * tq
    q_pos = s0 + lax.broadcasted_iota(jnp.int32, (1, tq), 1)
    q_pos4 = _tile_lanes(q_pos, GROUP)
    k_pos = s0 + lax.broadcasted_iota(jnp.int32, (tq, 1), 0)
    causal = _tile_lanes(jnp.where(k_pos <= q_pos, 0.0, MASKED), GROUP)
    n_idx = lax.broadcasted_iota(jnp.int32, (ncp, 1), 0)
    valid = (n_idx * CMP_STRIDE + (CMP_BLOCK - 1) <= q_pos4) & (n_idx < n_cmp)
    j_blk = lax.broadcasted_iota(jnp.int32, (nsp, ncp), 0)
    n_blk = lax.broadcasted_iota(jnp.int32, (nsp, ncp), 1)
    overlap = (jnp.minimum(n_blk * CMP_STRIDE + CMP_BLOCK, j_blk * SEL_BLOCK + SEL_BLOCK)
               - jnp.maximum(n_blk * CMP_STRIDE, j_blk * SEL_BLOCK))
    overlap = jnp.where((n_blk < n_cmp) & (j_blk < n_sel), jnp.maximum(overlap, 0), 0)
    weight = (overlap.astype(F32) * (1.0 / CMP_STRIDE)).astype(BF16)
    j_row = lax.broadcasted_iota(jnp.int32, (nsp, tq), 0).astype(F32)
    cur = jnp.right_shift(s0 + lax.broadcasted_iota(jnp.int32, (nsp, tq), 1), 6).astype(F32)
    forced = (j_row == 0) | (j_row == cur) | (j_row == cur - 1)
    gt_ref[...] = gate_ref[...].T

    def sel_scores(hp, kt):
        s = _dot_nt(ks_ref[hp, jnp.maximum(kt, 0)], qs_ref[hp])
        s_ref[hp] = s
        return jnp.max(s, axis=0, keepdims=True)

    def front(hp):
        heads = _query_heads(q_ref, hp)
        q_plain = _stack_queries(heads)

        sc = jnp.where(valid, _dot_nt(kc_ref[hp], q_plain), NEG_INF)
        mc = jnp.max(sc, axis=0, keepdims=True)
        mc = jnp.where(mc == NEG_INF, 0.0, mc)
        ec = jnp.where(valid, jnp.exp2(sc - mc), 0.0)
        den = jnp.sum(ec, axis=0, keepdims=True)
        pc = (ec / jnp.where(den > 0, den, 1.0)).astype(BF16)
        o_cmp = _dot(vc_ref[hp], pc)

        imp4 = _dot(weight, pc)
        imp = imp4[:, 0:tq]
        for g in range(1, GROUP):
            imp = imp + imp4[:, g * tq:(g + 1) * tq]
        score = jnp.where(forced, jnp.inf, jnp.where(j_row <= cur, imp, NEG_INF))
        chosen = jnp.zeros((nsp, tq), F32)
        for _ in range(min(SEL_TOPK, n_sel)):
            best = jnp.max(score, axis=0, keepdims=True)
            idx = jnp.min(jnp.where(score == best, j_row, float(nsp)), axis=0, keepdims=True)
            pick = j_row == idx
            chosen = jnp.where(pick, 1.0, chosen)
            score = jnp.where(pick, NEG_INF, score)

        pieces = [jnp.zeros((HEAD_DIM, tq), F32), jnp.where(chosen > 0.5, 0.0, MASKED)]
        if LANES - HEAD_DIM - nsp > 0:
            pieces.append(jnp.zeros((LANES - HEAD_DIM - nsp, tq), F32))
        qs_ref[hp] = _stack_queries(heads, jnp.concatenate(pieces, axis=0).T)

        _, acc_win = _window_chain(q_plain, kw_ref.at[hp], vw_ref.at[hp], qi, tq, NSA_WINDOW)

        s = _dot_nt(ks_ref[hp, qi], qs_ref[hp]) + causal
        m = jnp.max(s, axis=0, keepdims=True)
        p_ref[hp] = jnp.exp2(s - m).astype(BF16)
        acc_ref[hp] = jnp.zeros((ACC_ROWS, GROUP * tq), F32)
        return o_cmp, acc_win, (m, jnp.ones_like(m), sel_scores(hp, qi - 1))

    fronts = [front(hp) for hp in range(hps)]

    def sel_body(j, carry):
        out = ()
        for hp in range(hps):
            m, alpha, s_max = carry[3 * hp:3 * hp + 3]
            acc_ref[hp] = alpha * acc_ref[hp] + _dot(vs_ref[hp, qi - j], p_ref[hp])
            m_new = jnp.maximum(m, s_max)
            p_ref[hp] = jnp.exp2(s_ref[hp] - m_new).astype(BF16)
            out += (m_new, jnp.exp2(m - m_new), sel_scores(hp, qi - 2 - j))
        return out

    final = lax.fori_loop(0, qi, sel_body, tuple(x for f in fronts for x in f[2]))

    def normalised(a):
        return a[0:HEAD_DIM] / a[HEAD_DIM:HEAD_DIM + 1]

    for hp in range(hps):
        o_cmp, acc_win, _ = fronts[hp]
        alpha = final[3 * hp + 1]
        acc_sel = alpha * acc_ref[hp] + _dot(vs_ref[hp, 0], p_ref[hp])

        def gate_row(branch):
            first = ((hb * hps + hp) * GROUP) * 3 + branch
            return jnp.concatenate([gt_ref[pl.ds(first + 3 * g, 1), :] for g in range(GROUP)], axis=1)

        o_t = gate_row(0) * o_cmp + gate_row(1) * normalised(acc_sel) + gate_row(2) * normalised(acc_win)
        o_ref[:, hp * SLAB:(hp + 1) * SLAB] = _unstack_output(o_t, tq).astype(o_ref.dtype)


def _nsa_attention(q, kvb, kvcc, gate, batch, seq, n_cmp, n_sel):
    t, qd = q.shape
    tq = min(seq, 256)
    nq = seq // tq
    ncp = kvcc.shape[3]
    hps = NSA_HEADS_PER_STEP
    assert HEAD_DIM + n_sel <= NULL_LANE and tq % SEL_BLOCK == 0 and hps % 2 == 0
    kern = functools.partial(_nsa_attn_kernel, n_cmp=n_cmp, n_sel=n_sel, hps=hps)
    kv_lanes = hps // 2 * LANES
    blocks = NSA_KV_HEADS * HEAD_DIM // kv_lanes

    def kv_block(which):
        return pl.BlockSpec((seq, kv_lanes), lambda b, h, i: (b, which * blocks + h))

    def per_head(shape):
        return pltpu.VMEM((hps,) + shape, BF16)

    return pl.pallas_call(
        kern,
        grid=(batch, NSA_KV_HEADS // hps, nq),
        in_specs=[pl.BlockSpec((tq, hps * SLAB), lambda b, h, i: (b * nq + i, h)),
                  kv_block(0), kv_block(1), kv_block(2), kv_block(3),
                  pl.BlockSpec((1, 1, hps, ncp, LANES), lambda b, h, i: (b, 0, h, 0, 0)),
                  pl.BlockSpec((1, 1, hps, ncp, LANES), lambda b, h, i: (b, 1, h, 0, 0)),
                  pl.BlockSpec((tq, LANES), lambda b, h, i: (b * nq + i, 0))],
        out_specs=pl.BlockSpec((tq, hps * SLAB), lambda b, h, i: (b * nq + i, h)),
        out_shape=jax.ShapeDtypeStruct((t, qd), BF16),
        scratch_shapes=[per_head((nq + 1, tq, LANES)), per_head((nq + 1, ACC_ROWS, tq)),
                        per_head((nq + 1, tq, LANES)), per_head((nq + 1, ACC_ROWS, tq)),
                        per_head((ncp, LANES)), per_head((HEAD_DIM, ncp)),
                        per_head((GROUP * tq, LANES)),
                        pltpu.VMEM((hps, tq, GROUP * tq), F32), per_head((tq, GROUP * tq)),
                        pltpu.VMEM((hps, ACC_ROWS, GROUP * tq), F32),
                        pltpu.VMEM((LANES, tq), F32)],
        compiler_params=_params(("arbitrary", "arbitrary", "arbitrary")),
        name="nsa_attention",
    )(q, kvb, kvb, kvb, kvb, kvcc, kvcc, gate)


def _swa_attn_kernel(sink_ref, q_ref, ksrc_ref, vsrc_ref, o_ref, k_ref, v_ref):
    qi = pl.program_id(1)
    tq = q_ref.shape[0]
    n_slabs = q_ref.shape[1] // SLAB
    slabs_per_kv = n_slabs // SWA_KV_HEADS

    @pl.when(qi == 0)
    def _():
        for kvh in range(SWA_KV_HEADS):
            _build_kv_tiles(ksrc_ref, vsrc_ref, 0, kvh == 1, False, k_ref.at[kvh], v_ref.at[kvh])

    for hv in range(n_slabs):
        kvh = hv // slabs_per_kv
        q_stack = _stack_queries(_query_heads(q_ref, hv))
        sink = jnp.concatenate(
            [jnp.full((1, tq), sink_ref[hv * GROUP + g] * LOG2E, F32) for g in range(GROUP)], axis=1)
        m, acc = _window_chain(q_stack, k_ref.at[kvh], v_ref.at[kvh], qi, tq, SWA_WINDOW, floor=sink)
        denom = acc[HEAD_DIM:HEAD_DIM + 1] + jnp.exp2(sink - m)
        o_ref[:, hv * SLAB:(hv + 1) * SLAB] = _unstack_output(acc[0:HEAD_DIM] / denom, tq).astype(o_ref.dtype)


def _swa_attention(q, kvb, sinks, batch, seq):
    t, qd = q.shape
    tq = min(seq, 256)
    nq = seq // tq
    assert SWA_KV_HEADS * HEAD_DIM == LANES and SWA_KV_HEADS == 2
    return pl.pallas_call(
        _swa_attn_kernel,
        grid=(batch, nq),
        in_specs=[pl.BlockSpec(memory_space=pltpu.SMEM),
                  pl.BlockSpec((tq, qd), lambda b, i: (b * nq + i, 0)),
                  pl.BlockSpec((seq, LANES), lambda b, i: (b, 0)),
                  pl.BlockSpec((seq, LANES), lambda b, i: (b, 1))],
        out_specs=pl.BlockSpec((tq, qd), lambda b, i: (b * nq + i, 0)),
        out_shape=jax.ShapeDtypeStruct((t, qd), BF16),
        scratch_shapes=[pltpu.VMEM((SWA_KV_HEADS, nq + 1, tq, LANES), BF16),
                        pltpu.VMEM((SWA_KV_HEADS, nq + 1, ACC_ROWS, tq), BF16)],
        compiler_params=_params(("arbitrary", "arbitrary")),
        name="swa_attention",
    )(sinks, q, kvb, kvb)


def _post_kernel(o_ref, x_ref, g_ref, gate1_ref, sh2_ref, sc2_ref, gate2_ref,
                 wo_ref, wup_ref, wdn_ref, out_ref):
    y = _dot(o_ref[...], wo_ref[...])
    x1 = x_ref[...] + (1.0 + gate1_ref[0]) * _rms(y, g_ref[1])
    hb = (_rms(x1, g_ref[2]) * (1.0 + sc2_ref[0]) + sh2_ref[0]).astype(BF16)
    d = x1.shape[1]
    acc = jnp.zeros_like(x1)
    for c in range(wup_ref.shape[1] // d):
        a = jnp.maximum(_dot(hb, wup_ref[:, c * d:(c + 1) * d]), 0.0)
        acc = acc + _dot((a * a).astype(BF16), wdn_ref[c * d:(c + 1) * d, :])
    out_ref[...] = x1 + (1.0 + gate2_ref[0]) * _rms(acc, g_ref[3])


def _post_mixer(o, x2, mod_rows, norm_g4, layer, batch, seq, wo, wup, wdn):
    t, d = x2.shape
    tm = min(seq, 512)
    nt = seq // tm
    row = lambda i: (i, 0)
    mod_row = lambda part: pl.BlockSpec((1, 1, d), lambda i: ((layer * batch + i // nt) * 6 + part, 0, 0))
    return pl.pallas_call(
        _post_kernel,
        grid=(t // tm,),
        in_specs=[pl.BlockSpec((tm, o.shape[1]), row),
                  pl.BlockSpec((tm, d), row),
                  pl.BlockSpec((4, 1, d), lambda i: (layer, 0, 0)),
                  mod_row(2), mod_row(3), mod_row(4), mod_row(5),
                  _resident(wo.shape, lambda i: (0, 0)),
                  _resident(wup.shape, lambda i: (0, 0)),
                  _resident(wdn.shape, lambda i: (0, 0))],
        out_specs=pl.BlockSpec((tm, d), row),
        out_shape=jax.ShapeDtypeStruct((t, d), F32),
        compiler_params=_params(("arbitrary",)),
        name="post_mixer_mlp",
    )(o, x2, norm_g4, mod_rows, mod_rows, mod_rows, mod_rows, wo, wup, wdn)


def kernel(x, c, positions, ada_w, ada_b, norm_g, nsa_w_in, nsa_w_out, nsa_cmp_pe, nsa_phi_w1, nsa_phi_b1,
           nsa_phi_w2, nsa_phi_b2, swa_w_in, swa_w_out, swa_sinks, mlp_w_up, mlp_w_down):
    batch, seq, d = x.shape
    depth = ada_w.shape[0]
    assert seq % min(seq, 256) == 0 and seq % SEL_BLOCK == 0, "sequence must tile into query blocks"
    t = batch * seq
    n_cmp = (seq - CMP_BLOCK) // CMP_STRIDE + 1
    n_sel = seq // SEL_BLOCK
    ncp = -(-(seq // CMP_STRIDE) // LANES) * LANES

    cos_t, sin_t = _rope_tables(positions)
    mod = _modulation(c, ada_w, ada_b)
    mod_rows = mod.reshape(depth * batch * 6, 1, d)
    norm_rows = norm_g.reshape(depth * 4, 1, d)

    x2 = x.reshape(t, d)
    qd = d
    for i in range(depth):
        a = i // 2
        if i % 2 == 0:
            kd = NSA_KV_HEADS * HEAD_DIM
            w_in = nsa_w_in[a].astype(BF16)
            n_gate = w_in.shape[1] - qd - 6 * kd
            wg = jnp.pad(w_in[:, qd + 6 * kd:], ((0, 0), (0, LANES - n_gate)))
            q, kvc, kvb, gate = _pre_mixer("nsa", x2, mod_rows, norm_rows, i, batch, seq,
                                           (w_in[:, :qd], w_in[:, qd:qd + 6 * kd], wg), cos_t, sin_t)
            kvcc = _compress(kvc, batch, seq, nsa_cmp_pe[a], nsa_phi_w1[a], nsa_phi_b1[a],
                             nsa_phi_w2[a], nsa_phi_b2[a], ncp)
            o = _nsa_attention(q, kvb, kvcc, gate, batch, seq, n_cmp, n_sel)
            wo = nsa_w_out[a].astype(BF16)
        else:
            w_in = swa_w_in[a].astype(BF16)
            q, kvb = _pre_mixer("swa", x2, mod_rows, norm_rows, i, batch, seq,
                                (w_in[:, :qd], w_in[:, qd:]), cos_t, sin_t)
            o = _swa_attention(q, kvb, swa_sinks[a], batch, seq)
            wo = swa_w_out[a].astype(BF16)
        x2 = _post_mixer(o, x2, mod_rows, norm_rows, i, batch, seq, wo,
                         mlp_w_up[i].astype(BF16), mlp_w_down[i].astype(BF16))
    return x2.reshape(batch, seq, d)
```

```python
import functools

import jax
import jax.numpy as jnp
from jax import lax
from jax.experimental import pallas as pl
from jax.experimental.pallas import tpu as pltpu

F32 = jnp.float32
BF16 = jnp.bfloat16

HEAD_DIM = 64
ROPE_THETA = 10000.0
NORM_EPS = 1e-6
LOG2E = 1.4426950408889634
Q_SCALE = HEAD_DIM ** -0.5 * LOG2E
GROUP = 4
SLAB = GROUP * HEAD_DIM
LANES = 128
NSA_KV_HEADS = 4
NSA_WINDOW = 512
CMP_BLOCK = 32
CMP_STRIDE = 16
SEL_BLOCK = 64
SEL_TOPK = 8
SWA_KV_HEADS = 2
SWA_WINDOW = 128
VMEM_LIMIT = 56 * 1024 * 1024
NEG_INF = float("-inf")
MASKED = -1e30
ACC_ROWS = 80
NULL_LANE = LANES - 1
NSA_HEADS_PER_STEP = 4


def _dot(a, b):
    return jnp.dot(a, b, preferred_element_type=F32)


def _params(sem):
    return pltpu.CompilerParams(dimension_semantics=sem, vmem_limit_bytes=VMEM_LIMIT)


def _resident(shape, index_map):
    return pl.BlockSpec(shape, index_map, pipeline_mode=pl.Buffered(1))


def _rope_table_kernel(pos_ref, inv_ref, cos_ref, sin_ref):
    ang = pos_ref[...].astype(F32) * inv_ref[...]
    lane = lax.broadcasted_iota(jnp.int32, ang.shape, 1)
    first_half = (lane % HEAD_DIM) < HEAD_DIM // 2
    cos_ref[...] = jnp.cos(ang)
    s = jnp.sin(ang)
    sin_ref[...] = jnp.where(first_half, -s, s)


def _rope_tables(positions):
    t = positions.size
    tm = min(t, 2048)
    inv = 1.0 / (ROPE_THETA ** (jnp.arange(0, HEAD_DIM, 2, dtype=F32) / HEAD_DIM))
    inv_lanes = jnp.tile(inv, LANES // (HEAD_DIM // 2)).reshape(1, LANES)
    return pl.pallas_call(
        _rope_table_kernel,
        grid=(t // tm,),
        in_specs=[pl.BlockSpec((tm, 1), lambda i: (i, 0)),
                  pl.BlockSpec((1, LANES), lambda i: (0, 0))],
        out_specs=[pl.BlockSpec((tm, LANES), lambda i: (i, 0)),
                   pl.BlockSpec((tm, LANES), lambda i: (i, 0))],
        out_shape=[jax.ShapeDtypeStruct((t, LANES), F32)] * 2,
        compiler_params=_params(("arbitrary",)),
        name="rope_tables",
    )(positions.reshape(t, 1), inv_lanes)


def _rope(v, cos, sin_signed):
    lane = lax.broadcasted_iota(jnp.int32, (v.shape[0], LANES), 1)
    first_half = (lane % HEAD_DIM) < HEAD_DIM // 2
    outs = []
    for j in range(v.shape[1] // LANES):
        s = v[:, j * LANES:(j + 1) * LANES]
        partner = jnp.where(first_half,
                            pltpu.roll(s, LANES - HEAD_DIM // 2, 1),
                            pltpu.roll(s, HEAD_DIM // 2, 1))
        outs.append(s * cos + partner * sin_signed)
    return outs[0] if len(outs) == 1 else jnp.concatenate(outs, axis=1)


def _mod_kernel(c_ref, w_ref, b_ref, o_ref):
    c = c_ref[...]
    cond = (c * jax.nn.sigmoid(c)).astype(BF16)
    o_ref[0] = _dot(cond, w_ref[0].astype(BF16)) + b_ref[0]


def _modulation(c, ada_w, ada_b):
    depth, d, n = ada_w.shape
    b = c.shape[0]
    tn = 1024
    return pl.pallas_call(
        _mod_kernel,
        grid=(depth, n // tn),
        in_specs=[pl.BlockSpec((b, d), lambda l, j: (0, 0)),
                  pl.BlockSpec((1, d, tn), lambda l, j: (l, 0, j)),
                  pl.BlockSpec((1, 1, tn), lambda l, j: (l, 0, j))],
        out_specs=pl.BlockSpec((1, b, tn), lambda l, j: (l, 0, j)),
        out_shape=jax.ShapeDtypeStruct((depth, b, n), F32),
        compiler_params=_params(("arbitrary", "arbitrary")),
        name="adaln_modulation",
    )(c, ada_w, ada_b.reshape(depth, 1, n))


def _rms(x, g):
    ms = jnp.mean(x * x, axis=-1, keepdims=True)
    return x * lax.rsqrt(ms + NORM_EPS) * g


def _store_slabs(ref, x):
    for s in range(ref.shape[1]):
        ref[0, s] = x[:, s * SLAB:(s + 1) * SLAB].astype(ref.dtype)


def _load_slabs(ref):
    return jnp.concatenate([ref[0, s] for s in range(ref.shape[1])], axis=1)


def _nsa_pre_kernel(x_ref, g_ref, sh_ref, sc_ref, wq_ref, wkv_ref, wg_ref, cos_ref, sin_ref,
                    q_ref, kvc_ref, kvb_ref, gate_ref):
    h = _rms(x_ref[...], g_ref[0]) * (1.0 + sc_ref[0]) + sh_ref[0]
    hb = h.astype(BF16)
    cos = cos_ref[...]
    sin = sin_ref[...]
    _store_slabs(q_ref, _rope(_dot(hb, wq_ref[...]), cos, sin) * Q_SCALE)
    kv = _dot(hb, wkv_ref[...])
    kd = kvc_ref.shape[1] // 2
    kc = _rope(kv[:, 0:kd], cos, sin)
    kvc_ref[...] = jnp.concatenate([kc, kv[:, kd:2 * kd]], axis=1)
    ks = _rope(kv[:, 2 * kd:3 * kd], cos, sin)
    kw = _rope(kv[:, 4 * kd:5 * kd], cos, sin)
    kvb_ref[...] = jnp.concatenate(
        [ks, kv[:, 3 * kd:4 * kd], kw, kv[:, 5 * kd:6 * kd]], axis=1).astype(BF16)
    gate_ref[...] = jax.nn.sigmoid(_dot(hb, wg_ref[...]))


def _swa_pre_kernel(x_ref, g_ref, sh_ref, sc_ref, wq_ref, wkv_ref, cos_ref, sin_ref,
                    q_ref, kvb_ref):
    h = _rms(x_ref[...], g_ref[0]) * (1.0 + sc_ref[0]) + sh_ref[0]
    hb = h.astype(BF16)
    cos = cos_ref[...]
    sin = sin_ref[...]
    _store_slabs(q_ref, _rope(_dot(hb, wq_ref[...]), cos, sin) * Q_SCALE)
    kv = _dot(hb, wkv_ref[...])
    kd = kvb_ref.shape[1] // 2
    k = _rope(kv[:, 0:kd], cos, sin)
    kvb_ref[...] = jnp.concatenate([k, kv[:, kd:]], axis=1).astype(BF16)


def _pre_mixer(kind, x2, mod_rows, norm_rows, layer, batch, seq, weights, cos_t, sin_t):
    t, d = x2.shape
    tm = min(seq, 512)
    nt = seq // tm
    row = lambda b, i: (b * nt + i, 0)
    mod_row = lambda part: pl.BlockSpec((1, 1, d), lambda b, i: ((layer * batch + b) * 6 + part, 0, 0))
    common_in = [pl.BlockSpec((tm, d), row),
                 pl.BlockSpec((1, 1, d), lambda b, i: (layer * 4 + 0, 0, 0)),
                 mod_row(0), mod_row(1)]
    tab = [pl.BlockSpec((tm, LANES), row), pl.BlockSpec((tm, LANES), row)]
    q_spec = pl.BlockSpec((1, d // SLAB, tm, SLAB), lambda b, i: (b, 0, i, 0))
    q_shape = jax.ShapeDtypeStruct((batch, d // SLAB, seq, SLAB), BF16)
    if kind == "nsa":
        wq, wkv, wg = weights
        kd = NSA_KV_HEADS * HEAD_DIM
        return pl.pallas_call(
            _nsa_pre_kernel,
            grid=(batch, nt),
            in_specs=common_in + [_resident(wq.shape, lambda b, i: (0, 0)),
                                  _resident(wkv.shape, lambda b, i: (0, 0)),
                                  _resident(wg.shape, lambda b, i: (0, 0))] + tab,
            out_specs=[q_spec,
                       pl.BlockSpec((tm, 2 * kd), row),
                       pl.BlockSpec((tm, 4 * kd), row),
                       pl.BlockSpec((tm, LANES), row)],
            out_shape=[q_shape,
                       jax.ShapeDtypeStruct((t, 2 * kd), F32),
                       jax.ShapeDtypeStruct((t, 4 * kd), BF16),
                       jax.ShapeDtypeStruct((t, LANES), F32)],
            compiler_params=_params(("arbitrary", "arbitrary")),
            name="nsa_pre",
        )(x2, norm_rows, mod_rows, mod_rows, wq, wkv, wg, cos_t, sin_t)
    wq, wkv = weights
    kd = SWA_KV_HEADS * HEAD_DIM
    return pl.pallas_call(
        _swa_pre_kernel,
        grid=(batch, nt),
        in_specs=common_in + [_resident(wq.shape, lambda b, i: (0, 0)),
                              _resident(wkv.shape, lambda b, i: (0, 0))] + tab,
        out_specs=[q_spec,
                   pl.BlockSpec((tm, 2 * kd), row)],
        out_shape=[q_shape,
                   jax.ShapeDtypeStruct((t, 2 * kd), BF16)],
        compiler_params=_params(("arbitrary", "arbitrary")),
        name="swa_pre",
    )(x2, norm_rows, mod_rows, mod_rows, wq, wkv, cos_t, sin_t)


def _compress_kernel(x_ref, pe_ref, w1_ref, b1_ref, w2_ref, b2_ref, o_ref):
    nc = x_ref.shape[0] // CMP_STRIDE
    ncp = o_ref.shape[3]

    def chunk_rows(half):
        return jnp.concatenate(
            [x_ref[pl.ds(l, nc, stride=CMP_STRIDE), :] + pe_ref[0, CMP_STRIDE * half + l]
             for l in range(CMP_STRIDE)], axis=1).astype(BF16)

    lo = _dot(chunk_rows(0), w1_ref[0, 0])
    hi = _dot(chunk_rows(1), w1_ref[0, 1])
    pre = lo + pltpu.roll(hi, nc - 1, 0) + b1_ref[0]
    hid = jax.nn.gelu(pre, approximate=True).astype(BF16)
    out = _dot(hid, w2_ref[0]) + b2_ref[0]
    if ncp > nc:
        out = jnp.concatenate([out, jnp.zeros((ncp - nc, 2 * LANES), F32)], axis=0)
    o_ref[0, 0, 0] = out[:, 0:LANES]
    o_ref[0, 0, 1] = out[:, LANES:2 * LANES]


def _two_heads(w):
    eye = jnp.eye(2, dtype=w.dtype)
    blk = w[..., None, :, None, :] * eye[:, None, :, None]
    return blk.reshape(w.shape[:-2] + (2 * w.shape[-2], 2 * w.shape[-1]))


def _compress(kvc, batch, seq, pe, w1, b1, w2, b2, ncp):
    hk = NSA_KV_HEADS
    slabs = hk * HEAD_DIM // LANES
    hidden = w1.shape[-1]
    pad = LANES - HEAD_DIM
    half = CMP_BLOCK // 2
    w1p = _two_heads(w1.astype(BF16).reshape(2, 2, half, HEAD_DIM, hidden))
    w1p = w1p.reshape(2, 2, half * LANES, 2 * hidden)
    w2p = _two_heads(jnp.pad(w2.astype(BF16), ((0, 0), (0, 0), (0, pad))))
    b2p = jnp.pad(b2, ((0, 0), (0, pad)))
    return pl.pallas_call(
        _compress_kernel,
        grid=(2, batch, slabs),
        in_specs=[pl.BlockSpec((seq, LANES), lambda j, b, s: (b, j * slabs + s)),
                  pl.BlockSpec((1, CMP_BLOCK, 1, LANES), lambda j, b, s: (j, 0, 0, 0)),
                  pl.BlockSpec((1, 2, half * LANES, 2 * hidden), lambda j, b, s: (j, 0, 0, 0)),
                  pl.BlockSpec((1, 1, 2 * hidden), lambda j, b, s: (j, 0, 0)),
                  pl.BlockSpec((1, 2 * hidden, 2 * LANES), lambda j, b, s: (j, 0, 0)),
                  pl.BlockSpec((1, 1, 2 * LANES), lambda j, b, s: (j, 0, 0))],
        out_specs=pl.BlockSpec((1, 1, 2, ncp, LANES), lambda j, b, s: (b, j, s, 0, 0)),
        out_shape=jax.ShapeDtypeStruct((batch, 2, hk, ncp, LANES), F32),
        compiler_params=_params(("arbitrary", "arbitrary", "arbitrary")),
        name="nsa_compress",
    )(kvc, jnp.tile(pe, (1, 1, 2)).reshape(2, CMP_BLOCK, 1, LANES), w1p,
      jnp.tile(b1, (1, 2)).reshape(2, 1, 2 * hidden), w2p, jnp.tile(b2p, (1, 2)).reshape(2, 1, 2 * LANES))


def _tile_lanes(a, n):
    return jnp.concatenate([a] * n, axis=1)


def _lane(rows):
    return lax.broadcasted_iota(jnp.int32, (rows, LANES), 1)


def _query_heads(q_slab):
    qf = q_slab.astype(F32)
    heads = []
    for g in range(GROUP):
        x = qf[:, (g // 2) * LANES:(g // 2 + 1) * LANES]
        heads.append(pltpu.roll(x, HEAD_DIM, 1) if g % 2 else x)
    return heads


def _stack_queries(heads, extra=None):
    row = lax.broadcasted_iota(jnp.int32, (LANES, heads[0].shape[0]), 0)
    tail = jnp.where(row == NULL_LANE, MASKED, 0.0 if extra is None else extra)
    return jnp.concatenate([jnp.where(row < HEAD_DIM, x.T, tail).astype(BF16) for x in heads], axis=1)


def _build_kv_tiles(k_src, v_src, col, odd, block_lanes, k_ref, vt_ref, null_first=False):
    nt, tq = k_ref.shape[0] - 1, k_ref.shape[1]
    first = 1 if null_first else 0
    null_tile = 0 if null_first else nt
    lane = _lane(tq)
    row = lax.broadcasted_iota(jnp.int32, (tq, LANES), 0)
    ones_lane = jnp.where(lane == HEAD_DIM, 1.0, 0.0)
    for t in range(nt):
        kx = k_src[t * tq:(t + 1) * tq, col:col + LANES].astype(F32)
        if odd:
            kx = pltpu.roll(kx, HEAD_DIM, 1)
        if block_lanes:
            block = t * (tq // SEL_BLOCK) + row // SEL_BLOCK
            extra = jnp.where(lane == HEAD_DIM + block, 1.0, 0.0)
        else:
            extra = 0.0
        k_ref[first + t] = jnp.where(lane < HEAD_DIM, kx, extra).astype(BF16)
        vx = v_src[t * tq:(t + 1) * tq, col:col + LANES].astype(F32)
        if odd:
            vx = pltpu.roll(vx, HEAD_DIM, 1)
        vx = jnp.where(lane < HEAD_DIM, vx, ones_lane)
        vt_ref[first + t] = vx.T[0:ACC_ROWS].astype(BF16)
    k_ref[null_tile] = jnp.where(lane == NULL_LANE, 1.0, 0.0).astype(BF16)
    vt_ref[null_tile] = jnp.zeros(vt_ref.shape[1:], BF16)


def _unstack_output(o_t, tq):
    halves = []
    for p in range(GROUP // 2):
        pair = jnp.concatenate([o_t[:, (2 * p) * tq:(2 * p + 1) * tq],
                                o_t[:, (2 * p + 1) * tq:(2 * p + 2) * tq]], axis=0)
        halves.append(pair.T)
    return jnp.concatenate(halves, axis=1)


def _band_bias(q_pos, k_pos, window):
    d = q_pos - k_pos
    return _tile_lanes(jnp.where((d >= 0) & (d < window), 0.0, MASKED), GROUP)


def _first_tile(s, v_t):
    m = jnp.max(s, axis=0, keepdims=True)
    return m, _dot(v_t, jnp.exp2(s - m).astype(BF16))


def _next_tile(m, acc, s, v_t):
    m_new = jnp.maximum(m, jnp.max(s, axis=0, keepdims=True))
    alpha = jnp.exp2(m - m_new)
    return m_new, alpha * acc + _dot(v_t, jnp.exp2(s - m_new).astype(BF16))


def _window_chain(q_stack, k_ref, vt_ref, qi, tq, window):
    null_tile = k_ref.shape[0] - 1
    q_pos = qi * tq + lax.broadcasted_iota(jnp.int32, (1, tq), 1)
    k_pos = qi * tq + lax.broadcasted_iota(jnp.int32, (tq, 1), 0)
    s = _dot(k_ref[qi], q_stack) + _band_bias(q_pos, k_pos, window)
    m, acc = _first_tile(s, vt_ref[qi])
    keep = min(window, tq)
    for c in range(1, -(-window // tq) + 1):
        kt = jnp.where(qi >= c, qi - c, null_tile)
        s = _dot(k_ref[kt, tq - keep:tq, :], q_stack)
        if c * tq + keep - 1 >= window:
            k_pos = (qi - c) * tq + (tq - keep) + lax.broadcasted_iota(jnp.int32, (keep, 1), 0)
            s = s + _band_bias(q_pos, k_pos, window)
        m, acc = _next_tile(m, acc, s, vt_ref[kt, :, tq - keep:tq])
    return m, acc


def _nsa_attn_kernel(q_ref, ksrc_ref, vsrc_ref, kwsrc_ref, vwsrc_ref, kcc_ref, vcc_ref, gate_ref, o_ref,
                     ks_ref, vs_ref, kw_ref, vw_ref, kc_ref, vc_ref, qs_ref, s_ref, p_ref, acc_ref, gt_ref,
                     *, n_cmp, n_sel, hps):
    hb = pl.program_id(1)
    qi = pl.program_id(2)
    tq = q_ref.shape[2]
    ncp = kc_ref.shape[1]
    nsp = -(-n_sel // 8) * 8

    @pl.when(qi == 0)
    def _():
        for hp in range(hps):
            col, odd = (hp // 2) * LANES, hp % 2 == 1
            _build_kv_tiles(ksrc_ref, vsrc_ref, col, odd, True, ks_ref.at[hp], vs_ref.at[hp])
            _build_kv_tiles(kwsrc_ref, vwsrc_ref, col, odd, False, kw_ref.at[hp], vw_ref.at[hp])
            kc_ref[hp] = kcc_ref[0, 0, hp].astype(BF16)
            vc_ref[hp] = vcc_ref[0, 0, hp].T[0:HEAD_DIM].astype(BF16)

    s0 = qi * tq
    q_pos = s0 + lax.broadcasted_iota(jnp.int32, (1, tq), 1)
    q_pos4 = _tile_lanes(q_pos, GROUP)
    k_pos = s0 + lax.broadcasted_iota(jnp.int32, (tq, 1), 0)
    causal = _tile_lanes(jnp.where(k_pos <= q_pos, 0.0, MASKED), GROUP)
    n_idx = lax.broadcasted_iota(jnp.int32, (ncp, 1), 0)
    valid = (n_idx * CMP_STRIDE + (CMP_BLOCK - 1) <= q_pos4) & (n_idx < n_cmp)
    j_blk = lax.broadcasted_iota(jnp.int32, (nsp, ncp), 0)
    n_blk = lax.broadcasted_iota(jnp.int32, (nsp, ncp), 1)
    overlap = (jnp.minimum(n_blk * CMP_STRIDE + CMP_BLOCK, j_blk * SEL_BLOCK + SEL_BLOCK)
               - jnp.maximum(n_blk * CMP_STRIDE, j_blk * SEL_BLOCK))
    overlap = jnp.where((n_blk < n_cmp) & (j_blk < n_sel), jnp.maximum(overlap, 0), 0)
    weight = (overlap.astype(F32) * (1.0 / CMP_STRIDE)).astype(BF16)
    j_row = lax.broadcasted_iota(jnp.int32, (nsp, tq), 0).astype(F32)
    cur = jnp.right_shift(s0 + lax.broadcasted_iota(jnp.int32, (nsp, tq), 1), 6).astype(F32)
    forced = (j_row == 0) | (j_row == cur) | (j_row == cur - 1)
    gt_ref[...] = gate_ref[...].T

    def sel_scores(hp, kt):
        s = _dot(ks_ref[hp, jnp.maximum(kt, 0)], qs_ref[hp])
        s_ref[hp] = s
        return jnp.max(s, axis=0, keepdims=True)

    def front(hp):
        heads = _query_heads(q_ref[0, hp])
        q_plain = _stack_queries(heads)

        sc = jnp.where(valid, _dot(kc_ref[hp], q_plain), NEG_INF)
        mc = jnp.max(sc, axis=0, keepdims=True)
        mc = jnp.where(mc == NEG_INF, 0.0, mc)
        ec = jnp.where(valid, jnp.exp2(sc - mc), 0.0)
        den = jnp.sum(ec, axis=0, keepdims=True)
        pc = (ec / jnp.where(den > 0, den, 1.0)).astype(BF16)
        o_cmp = _dot(vc_ref[hp], pc)

        imp4 = _dot(weight, pc)
        imp = imp4[:, 0:tq]
        for g in range(1, GROUP):
            imp = imp + imp4[:, g * tq:(g + 1) * tq]
        score = jnp.where(forced, jnp.inf, jnp.where(j_row <= cur, imp, NEG_INF))
        chosen = jnp.zeros((nsp, tq), F32)
        for _ in range(min(SEL_TOPK, n_sel)):
            best = jnp.max(score, axis=0, keepdims=True)
            idx = jnp.min(jnp.where(score == best, j_row, float(nsp)), axis=0, keepdims=True)
            pick = j_row == idx
            chosen = jnp.where(pick, 1.0, chosen)
            score = jnp.where(pick, NEG_INF, score)

        pieces = [jnp.zeros((HEAD_DIM, tq), F32), jnp.where(chosen > 0.5, 0.0, MASKED)]
        if LANES - HEAD_DIM - nsp > 0:
            pieces.append(jnp.zeros((LANES - HEAD_DIM - nsp, tq), F32))
        qs_ref[hp] = _stack_queries(heads, jnp.concatenate(pieces, axis=0))

        _, acc_win = _window_chain(q_plain, kw_ref.at[hp], vw_ref.at[hp], qi, tq, NSA_WINDOW)

        s = _dot(ks_ref[hp, qi], qs_ref[hp]) + causal
        m = jnp.max(s, axis=0, keepdims=True)
        p_ref[hp] = jnp.exp2(s - m).astype(BF16)
        acc_ref[hp] = jnp.zeros((ACC_ROWS, GROUP * tq), F32)
        return o_cmp, acc_win, (m, jnp.ones_like(m), sel_scores(hp, qi - 1))

    fronts = [front(hp) for hp in range(hps)]

    def sel_body(j, carry):
        out = ()
        for hp in range(hps):
            m, alpha, s_max = carry[3 * hp:3 * hp + 3]
            acc_ref[hp] = alpha * acc_ref[hp] + _dot(vs_ref[hp, qi - j], p_ref[hp])
            m_new = jnp.maximum(m, s_max)
            p_ref[hp] = jnp.exp2(s_ref[hp] - m_new).astype(BF16)
            out += (m_new, jnp.exp2(m - m_new), sel_scores(hp, qi - 2 - j))
        return out

    final = lax.fori_loop(0, qi, sel_body, tuple(x for f in fronts for x in f[2]))

    def normalised(a):
        return a[0:HEAD_DIM] / a[HEAD_DIM:HEAD_DIM + 1]

    for hp in range(hps):
        o_cmp, acc_win, _ = fronts[hp]
        alpha = final[3 * hp + 1]
        acc_sel = alpha * acc_ref[hp] + _dot(vs_ref[hp, 0], p_ref[hp])

        def gate_row(branch):
            first = ((hb * hps + hp) * GROUP) * 3 + branch
            return jnp.concatenate([gt_ref[pl.ds(first + 3 * g, 1), :] for g in range(GROUP)], axis=1)

        o_t = gate_row(0) * o_cmp + gate_row(1) * normalised(acc_sel) + gate_row(2) * normalised(acc_win)
        o_ref[0, hp] = _unstack_output(o_t, tq).astype(o_ref.dtype)


def _nsa_attention(q, kvb, kvcc, gate, batch, seq, n_cmp, n_sel):
    tq = min(seq, 256)
    nq = seq // tq
    ncp = kvcc.shape[3]
    hps = NSA_HEADS_PER_STEP
    assert HEAD_DIM + n_sel <= NULL_LANE and tq % SEL_BLOCK == 0 and hps % 2 == 0
    kern = functools.partial(_nsa_attn_kernel, n_cmp=n_cmp, n_sel=n_sel, hps=hps)
    kv_lanes = hps // 2 * LANES
    blocks = NSA_KV_HEADS * HEAD_DIM // kv_lanes

    def kv_block(which):
        return pl.BlockSpec((seq, kv_lanes), lambda b, h, i: (b, which * blocks + h))

    def per_head(shape):
        return pltpu.VMEM((hps,) + shape, BF16)

    return pl.pallas_call(
        kern,
        grid=(batch, NSA_KV_HEADS // hps, nq),
        in_specs=[pl.BlockSpec((1, hps, tq, SLAB), lambda b, h, i: (b, h, i, 0)),
                  kv_block(0), kv_block(1), kv_block(2), kv_block(3),
                  pl.BlockSpec((1, 1, hps, ncp, LANES), lambda b, h, i: (b, 0, h, 0, 0)),
                  pl.BlockSpec((1, 1, hps, ncp, LANES), lambda b, h, i: (b, 1, h, 0, 0)),
                  pl.BlockSpec((tq, LANES), lambda b, h, i: (b * nq + i, 0))],
        out_specs=pl.BlockSpec((1, hps, tq, SLAB), lambda b, h, i: (b, h, i, 0)),
        out_shape=jax.ShapeDtypeStruct(q.shape, BF16),
        scratch_shapes=[per_head((nq + 1, tq, LANES)), per_head((nq + 1, ACC_ROWS, tq)),
                        per_head((nq + 1, tq, LANES)), per_head((nq + 1, ACC_ROWS, tq)),
                        per_head((ncp, LANES)), per_head((HEAD_DIM, ncp)),
                        per_head((LANES, GROUP * tq)),
                        pltpu.VMEM((hps, tq, GROUP * tq), F32), per_head((tq, GROUP * tq)),
                        pltpu.VMEM((hps, ACC_ROWS, GROUP * tq), F32),
                        pltpu.VMEM((LANES, tq), F32)],
        compiler_params=_params(("arbitrary", "arbitrary", "arbitrary")),
        name="nsa_attention",
    )(q, kvb, kvb, kvb, kvb, kvcc, kvcc, gate)


def _swa_attn_kernel(sink_ref, q_ref, ksrc_ref, vsrc_ref, o_ref, k_ref, v_ref, s_ref, p_ref):
    n_slabs, seq = q_ref.shape[1], q_ref.shape[2]
    tq = s_ref.shape[1] // GROUP
    w = k_ref.shape[2]
    reach = tq // w + 1
    slabs_per_kv = n_slabs // SWA_KV_HEADS
    n_items = seq // tq * n_slabs

    for kvh in range(SWA_KV_HEADS):
        _build_kv_tiles(ksrc_ref, vsrc_ref, 0, kvh == 1, False, k_ref.at[kvh], v_ref.at[kvh], null_first=True)

    k_pos = lax.broadcasted_iota(jnp.int32, (reach * w, 1), 0) - w
    bias = _band_bias(lax.broadcasted_iota(jnp.int32, (1, tq), 1), k_pos, w)

    def sink_row(slab):
        return jnp.concatenate(
            [jnp.full((1, tq), sink_ref[slab * GROUP + g] * LOG2E, F32) for g in range(GROUP)], axis=1)

    def scores(t):
        qi, slab = t // n_slabs, t % n_slabs
        q_stack = _stack_queries(_query_heads(q_ref[0, slab, pl.ds(qi * tq, tq), :]))
        keys = k_ref[slab // slabs_per_kv, pl.ds(qi * (tq // w), reach)].reshape(reach * w, LANES)
        s = _dot(keys, q_stack) + bias
        s_ref[...] = s
        return jnp.maximum(jnp.max(s, axis=0, keepdims=True), sink_row(slab))

    def probabilities(m):
        p_ref[...] = jnp.exp2(s_ref[...] - m).astype(BF16)

    def values(t, m):
        qi, slab = t // n_slabs, t % n_slabs
        kvh, first = slab // slabs_per_kv, qi * (tq // w)
        acc = _dot(v_ref[kvh, first], p_ref[0:w, :])
        for i in range(1, reach):
            acc = acc + _dot(v_ref[kvh, first + i], p_ref[i * w:(i + 1) * w, :])
        denom = acc[HEAD_DIM:HEAD_DIM + 1] + jnp.exp2(sink_row(slab) - m)
        o_ref[0, slab, pl.ds(qi * tq, tq), :] = _unstack_output(acc[0:HEAD_DIM] / denom, tq).astype(o_ref.dtype)

    m_first = scores(0)
    probabilities(m_first)

    def body(j, carry):
        m_pending_values, m_pending_probabilities = carry
        values(j, m_pending_values)
        probabilities(m_pending_probabilities)
        return m_pending_probabilities, scores(jnp.minimum(j + 2, n_items - 1))

    lax.fori_loop(0, n_items, body, (m_first, scores(jnp.minimum(1, n_items - 1))))


def _swa_attention(q, kvb, sinks, batch, seq):
    n_slabs = q.shape[1]
    tq = min(seq, 256)
    w = SWA_WINDOW
    assert SWA_KV_HEADS * HEAD_DIM == LANES and SWA_KV_HEADS == 2 and tq % w == 0 and seq % tq == 0
    whole = pl.BlockSpec((1, n_slabs, seq, SLAB), lambda b: (b, 0, 0, 0))
    return pl.pallas_call(
        _swa_attn_kernel,
        grid=(batch,),
        in_specs=[pl.BlockSpec(memory_space=pltpu.SMEM), whole,
                  pl.BlockSpec((seq, LANES), lambda b: (b, 0)),
                  pl.BlockSpec((seq, LANES), lambda b: (b, 1))],
        out_specs=whole,
        out_shape=jax.ShapeDtypeStruct(q.shape, BF16),
        scratch_shapes=[pltpu.VMEM((SWA_KV_HEADS, seq // w + 1, w, LANES), BF16),
                        pltpu.VMEM((SWA_KV_HEADS, seq // w + 1, ACC_ROWS, w), BF16),
                        pltpu.VMEM(((tq // w + 1) * w, GROUP * tq), F32),
                        pltpu.VMEM(((tq // w + 1) * w, GROUP * tq), BF16)],
        compiler_params=_params(("arbitrary",)),
        name="swa_attention",
    )(sinks, q, kvb, kvb)


def _post_kernel(o_ref, x_ref, g_ref, gate1_ref, sh2_ref, sc2_ref, gate2_ref,
                 wo_ref, wup_ref, wdn_ref, out_ref):
    y = _dot(_load_slabs(o_ref), wo_ref[...])
    x1 = x_ref[...] + (1.0 + gate1_ref[0]) * _rms(y, g_ref[1])
    hb = (_rms(x1, g_ref[2]) * (1.0 + sc2_ref[0]) + sh2_ref[0]).astype(BF16)
    d = x1.shape[1]
    acc = jnp.zeros_like(x1)
    for c in range(wup_ref.shape[1] // d):
        a = jnp.maximum(_dot(hb, wup_ref[:, c * d:(c + 1) * d]), 0.0)
        acc = acc + _dot((a * a).astype(BF16), wdn_ref[c * d:(c + 1) * d, :])
    out_ref[...] = x1 + (1.0 + gate2_ref[0]) * _rms(acc, g_ref[3])


def _post_mixer(o, x2, mod_rows, norm_g4, layer, batch, seq, wo, wup, wdn):
    t, d = x2.shape
    tm = min(seq, 512)
    nt = seq // tm
    row = lambda i: (i, 0)
    mod_row = lambda part: pl.BlockSpec((1, 1, d), lambda i: ((layer * batch + i // nt) * 6 + part, 0, 0))
    return pl.pallas_call(
        _post_kernel,
        grid=(t // tm,),
        in_specs=[pl.BlockSpec((1, o.shape[1], tm, SLAB), lambda i: (i // nt, 0, i % nt, 0)),
                  pl.BlockSpec((tm, d), row),
                  pl.BlockSpec((4, 1, d), lambda i: (layer, 0, 0)),
                  mod_row(2), mod_row(3), mod_row(4), mod_row(5),
                  _resident(wo.shape, lambda i: (0, 0)),
                  _resident(wup.shape, lambda i: (0, 0)),
                  _resident(wdn.shape, lambda i: (0, 0))],
        out_specs=pl.BlockSpec((tm, d), row),
        out_shape=jax.ShapeDtypeStruct((t, d), F32),
        compiler_params=_params(("arbitrary",)),
        name="post_mixer_mlp",
    )(o, x2, norm_g4, mod_rows, mod_rows, mod_rows, mod_rows, wo, wup, wdn)


def kernel(x, c, positions, ada_w, ada_b, norm_g, nsa_w_in, nsa_w_out, nsa_cmp_pe, nsa_phi_w1, nsa_phi_b1,
           nsa_phi_w2, nsa_phi_b2, swa_w_in, swa_w_out, swa_sinks, mlp_w_up, mlp_w_down):
    batch, seq, d = x.shape
    depth = ada_w.shape[0]
    assert seq % min(seq, 256) == 0 and seq % SEL_BLOCK == 0, "sequence must tile into query blocks"
    t = batch * seq
    n_cmp = (seq - CMP_BLOCK) // CMP_STRIDE + 1
    n_sel = seq // SEL_BLOCK
    ncp = -(-(seq // CMP_STRIDE) // LANES) * LANES

    cos_t, sin_t = _rope_tables(positions)
    mod = _modulation(c, ada_w, ada_b)
    mod_rows = mod.reshape(depth * batch * 6, 1, d)
    norm_rows = norm_g.reshape(depth * 4, 1, d)

    x2 = x.reshape(t, d)
    qd = d
    for i in range(depth):
        a = i // 2
        if i % 2 == 0:
            kd = NSA_KV_HEADS * HEAD_DIM
            w_in = nsa_w_in[a].astype(BF16)
            n_gate = w_in.shape[1] - qd - 6 * kd
            wg = jnp.pad(w_in[:, qd + 6 * kd:], ((0, 0), (0, LANES - n_gate)))
            q, kvc, kvb, gate = _pre_mixer("nsa", x2, mod_rows, norm_rows, i, batch, seq,
                                           (w_in[:, :qd], w_in[:, qd:qd + 6 * kd], wg), cos_t, sin_t)
            kvcc = _compress(kvc, batch, seq, nsa_cmp_pe[a], nsa_phi_w1[a], nsa_phi_b1[a],
                             nsa_phi_w2[a], nsa_phi_b2[a], ncp)
            o = _nsa_attention(q, kvb, kvcc, gate, batch, seq, n_cmp, n_sel)
            wo = nsa_w_out[a].astype(BF16)
        else:
            w_in = swa_w_in[a].astype(BF16)
            q, kvb = _pre_mixer("swa", x2, mod_rows, norm_rows, i, batch, seq,
                                (w_in[:, :qd], w_in[:, qd:]), cos_t, sin_t)
            o = _swa_attention(q, kvb, swa_sinks[a], batch, seq)
            wo = swa_w_out[a].astype(BF16)
        x2 = _post_mixer(o, x2, mod_rows, norm_rows, i, batch, seq, wo,
                         mlp_w_up[i].astype(BF16), mlp_w_down[i].astype(BF16))
    return x2.reshape(batch, seq, d)
```

```python
import functools

import jax
import jax.numpy as jnp
from jax import lax
from jax.experimental import pallas as pl
from jax.experimental.pallas import tpu as pltpu

F32 = jnp.float32
BF16 = jnp.bfloat16

HEAD_DIM = 64
ROPE_THETA = 10000.0
NORM_EPS = 1e-6
LOG2E = 1.4426950408889634
Q_SCALE = HEAD_DIM ** -0.5 * LOG2E
GROUP = 4
SLAB = GROUP * HEAD_DIM
LANES = 128
NSA_KV_HEADS = 4
NSA_WINDOW = 512
CMP_BLOCK = 32
CMP_STRIDE = 16
SEL_BLOCK = 64
SEL_TOPK = 8
SWA_KV_HEADS = 2
SWA_WINDOW = 128
VMEM_LIMIT = 56 * 1024 * 1024
NEG_INF = float("-inf")
MASKED = -1e30
ACC_ROWS = 80
NULL_LANE = LANES - 1
NSA_HEADS_PER_STEP = 4


def _dot(a, b):
    return jnp.dot(a, b, preferred_element_type=F32)


def _params(sem):
    return pltpu.CompilerParams(dimension_semantics=sem, vmem_limit_bytes=VMEM_LIMIT)


def _resident(shape, index_map):
    return pl.BlockSpec(shape, index_map, pipeline_mode=pl.Buffered(1))


def _rope_table_kernel(pos_ref, inv_ref, cos_ref, sin_ref):
    ang = pos_ref[...].astype(F32) * inv_ref[...]
    lane = lax.broadcasted_iota(jnp.int32, ang.shape, 1)
    first_half = (lane % HEAD_DIM) < HEAD_DIM // 2
    cos_ref[...] = jnp.cos(ang)
    s = jnp.sin(ang)
    sin_ref[...] = jnp.where(first_half, -s, s)


def _rope_tables(positions):
    t = positions.size
    tm = min(t, 2048)
    inv = 1.0 / (ROPE_THETA ** (jnp.arange(0, HEAD_DIM, 2, dtype=F32) / HEAD_DIM))
    inv_lanes = jnp.tile(inv, LANES // (HEAD_DIM // 2)).reshape(1, LANES)
    return pl.pallas_call(
        _rope_table_kernel,
        grid=(t // tm,),
        in_specs=[pl.BlockSpec((tm, 1), lambda i: (i, 0)),
                  pl.BlockSpec((1, LANES), lambda i: (0, 0))],
        out_specs=[pl.BlockSpec((tm, LANES), lambda i: (i, 0)),
                   pl.BlockSpec((tm, LANES), lambda i: (i, 0))],
        out_shape=[jax.ShapeDtypeStruct((t, LANES), F32)] * 2,
        compiler_params=_params(("arbitrary",)),
        name="rope_tables",
    )(positions.reshape(t, 1), inv_lanes)


def _rope(v, cos, sin_signed):
    lane = lax.broadcasted_iota(jnp.int32, (v.shape[0], LANES), 1)
    first_half = (lane % HEAD_DIM) < HEAD_DIM // 2
    outs = []
    for j in range(v.shape[1] // LANES):
        s = v[:, j * LANES:(j + 1) * LANES]
        partner = jnp.where(first_half,
                            pltpu.roll(s, LANES - HEAD_DIM // 2, 1),
                            pltpu.roll(s, HEAD_DIM // 2, 1))
        outs.append(s * cos + partner * sin_signed)
    return outs[0] if len(outs) == 1 else jnp.concatenate(outs, axis=1)


def _mod_kernel(c_ref, w_ref, b_ref, o_ref):
    c = c_ref[...]
    cond = (c * jax.nn.sigmoid(c)).astype(BF16)
    o_ref[0] = _dot(cond, w_ref[0].astype(BF16)) + b_ref[0]


def _modulation(c, ada_w, ada_b):
    depth, d, n = ada_w.shape
    b = c.shape[0]
    tn = 1024
    return pl.pallas_call(
        _mod_kernel,
        grid=(depth, n // tn),
        in_specs=[pl.BlockSpec((b, d), lambda l, j: (0, 0)),
                  pl.BlockSpec((1, d, tn), lambda l, j: (l, 0, j)),
                  pl.BlockSpec((1, 1, tn), lambda l, j: (l, 0, j))],
        out_specs=pl.BlockSpec((1, b, tn), lambda l, j: (l, 0, j)),
        out_shape=jax.ShapeDtypeStruct((depth, b, n), F32),
        compiler_params=_params(("arbitrary", "arbitrary")),
        name="adaln_modulation",
    )(c, ada_w, ada_b.reshape(depth, 1, n))


def _rms(x, g):
    ms = jnp.mean(x * x, axis=-1, keepdims=True)
    return x * lax.rsqrt(ms + NORM_EPS) * g


def _store_slabs(ref, x):
    for s in range(ref.shape[1]):
        ref[0, s] = x[:, s * SLAB:(s + 1) * SLAB].astype(ref.dtype)


def _load_slabs(ref):
    return jnp.concatenate([ref[0, s] for s in range(ref.shape[1])], axis=1)


def _nsa_pre_kernel(x_ref, g_ref, sh_ref, sc_ref, wq_ref, wkv_ref, wg_ref, cos_ref, sin_ref,
                    q_ref, kvc_ref, kvb_ref, gate_ref):
    h = _rms(x_ref[...], g_ref[0]) * (1.0 + sc_ref[0]) + sh_ref[0]
    hb = h.astype(BF16)
    cos = cos_ref[...]
    sin = sin_ref[...]
    _store_slabs(q_ref, _rope(_dot(hb, wq_ref[...]), cos, sin) * Q_SCALE)
    kv = _dot(hb, wkv_ref[...])
    kd = kvc_ref.shape[1] // 2
    kc = _rope(kv[:, 0:kd], cos, sin)
    kvc_ref[...] = jnp.concatenate([kc, kv[:, kd:2 * kd]], axis=1)
    ks = _rope(kv[:, 2 * kd:3 * kd], cos, sin)
    kw = _rope(kv[:, 4 * kd:5 * kd], cos, sin)
    kvb_ref[...] = jnp.concatenate(
        [ks, kv[:, 3 * kd:4 * kd], kw, kv[:, 5 * kd:6 * kd]], axis=1).astype(BF16)
    gate_ref[...] = jax.nn.sigmoid(_dot(hb, wg_ref[...]))


def _swa_pre_kernel(x_ref, g_ref, sh_ref, sc_ref, wq_ref, wkv_ref, cos_ref, sin_ref,
                    q_ref, kvb_ref):
    h = _rms(x_ref[...], g_ref[0]) * (1.0 + sc_ref[0]) + sh_ref[0]
    hb = h.astype(BF16)
    cos = cos_ref[...]
    sin = sin_ref[...]
    _store_slabs(q_ref, _rope(_dot(hb, wq_ref[...]), cos, sin) * Q_SCALE)
    kv = _dot(hb, wkv_ref[...])
    kd = kvb_ref.shape[1] // 2
    k = _rope(kv[:, 0:kd], cos, sin)
    kvb_ref[...] = jnp.concatenate([k, kv[:, kd:]], axis=1).astype(BF16)


def _pre_mixer(kind, x2, mod_rows, norm_rows, layer, batch, seq, weights, cos_t, sin_t):
    t, d = x2.shape
    tm = min(seq, 512)
    nt = seq // tm
    row = lambda b, i: (b * nt + i, 0)
    mod_row = lambda part: pl.BlockSpec((1, 1, d), lambda b, i: ((layer * batch + b) * 6 + part, 0, 0))
    common_in = [pl.BlockSpec((tm, d), row),
                 pl.BlockSpec((1, 1, d), lambda b, i: (layer * 4 + 0, 0, 0)),
                 mod_row(0), mod_row(1)]
    tab = [pl.BlockSpec((tm, LANES), row), pl.BlockSpec((tm, LANES), row)]
    q_spec = pl.BlockSpec((1, d // SLAB, tm, SLAB), lambda b, i: (b, 0, i, 0))
    q_shape = jax.ShapeDtypeStruct((batch, d // SLAB, seq, SLAB), BF16)
    if kind == "nsa":
        wq, wkv, wg = weights
        kd = NSA_KV_HEADS * HEAD_DIM
        return pl.pallas_call(
            _nsa_pre_kernel,
            grid=(batch, nt),
            in_specs=common_in + [_resident(wq.shape, lambda b, i: (0, 0)),
                                  _resident(wkv.shape, lambda b, i: (0, 0)),
                                  _resident(wg.shape, lambda b, i: (0, 0))] + tab,
            out_specs=[q_spec,
                       pl.BlockSpec((tm, 2 * kd), row),
                       pl.BlockSpec((tm, 4 * kd), row),
                       pl.BlockSpec((tm, LANES), row)],
            out_shape=[q_shape,
                       jax.ShapeDtypeStruct((t, 2 * kd), F32),
                       jax.ShapeDtypeStruct((t, 4 * kd), BF16),
                       jax.ShapeDtypeStruct((t, LANES), F32)],
            compiler_params=_params(("arbitrary", "arbitrary")),
            name="nsa_pre",
        )(x2, norm_rows, mod_rows, mod_rows, wq, wkv, wg, cos_t, sin_t)
    wq, wkv = weights
    kd = SWA_KV_HEADS * HEAD_DIM
    return pl.pallas_call(
        _swa_pre_kernel,
        grid=(batch, nt),
        in_specs=common_in + [_resident(wq.shape, lambda b, i: (0, 0)),
                              _resident(wkv.shape, lambda b, i: (0, 0))] + tab,
        out_specs=[q_spec,
                   pl.BlockSpec((tm, 2 * kd), row)],
        out_shape=[q_shape,
                   jax.ShapeDtypeStruct((t, 2 * kd), BF16)],
        compiler_params=_params(("arbitrary", "arbitrary")),
        name="swa_pre",
    )(x2, norm_rows, mod_rows, mod_rows, wq, wkv, cos_t, sin_t)


def _compress_kernel(x_ref, pe_ref, w1_ref, b1_ref, w2_ref, b2_ref, o_ref):
    nc = x_ref.shape[0] // CMP_STRIDE
    ncp = o_ref.shape[3]

    def chunk_rows(half):
        return jnp.concatenate(
            [x_ref[pl.ds(l, nc, stride=CMP_STRIDE), :] + pe_ref[0, CMP_STRIDE * half + l]
             for l in range(CMP_STRIDE)], axis=1).astype(BF16)

    lo = _dot(chunk_rows(0), w1_ref[0, 0])
    hi = _dot(chunk_rows(1), w1_ref[0, 1])
    pre = lo + pltpu.roll(hi, nc - 1, 0) + b1_ref[0]
    hid = jax.nn.gelu(pre, approximate=True).astype(BF16)
    out = _dot(hid, w2_ref[0]) + b2_ref[0]
    if ncp > nc:
        out = jnp.concatenate([out, jnp.zeros((ncp - nc, 2 * LANES), F32)], axis=0)
    o_ref[0, 0, 0] = out[:, 0:LANES]
    o_ref[0, 0, 1] = out[:, LANES:2 * LANES]


def _two_heads(w):
    eye = jnp.eye(2, dtype=w.dtype)
    blk = w[..., None, :, None, :] * eye[:, None, :, None]
    return blk.reshape(w.shape[:-2] + (2 * w.shape[-2], 2 * w.shape[-1]))


def _compress(kvc, batch, seq, pe, w1, b1, w2, b2, ncp):
    hk = NSA_KV_HEADS
    slabs = hk * HEAD_DIM // LANES
    hidden = w1.shape[-1]
    pad = LANES - HEAD_DIM
    half = CMP_BLOCK // 2
    w1p = _two_heads(w1.astype(BF16).reshape(2, 2, half, HEAD_DIM, hidden))
    w1p = w1p.reshape(2, 2, half * LANES, 2 * hidden)
    w2p = _two_heads(jnp.pad(w2.astype(BF16), ((0, 0), (0, 0), (0, pad))))
    b2p = jnp.pad(b2, ((0, 0), (0, pad)))
    return pl.pallas_call(
        _compress_kernel,
        grid=(2, batch, slabs),
        in_specs=[pl.BlockSpec((seq, LANES), lambda j, b, s: (b, j * slabs + s)),
                  pl.BlockSpec((1, CMP_BLOCK, 1, LANES), lambda j, b, s: (j, 0, 0, 0)),
                  pl.BlockSpec((1, 2, half * LANES, 2 * hidden), lambda j, b, s: (j, 0, 0, 0)),
                  pl.BlockSpec((1, 1, 2 * hidden), lambda j, b, s: (j, 0, 0)),
                  pl.BlockSpec((1, 2 * hidden, 2 * LANES), lambda j, b, s: (j, 0, 0)),
                  pl.BlockSpec((1, 1, 2 * LANES), lambda j, b, s: (j, 0, 0))],
        out_specs=pl.BlockSpec((1, 1, 2, ncp, LANES), lambda j, b, s: (b, j, s, 0, 0)),
        out_shape=jax.ShapeDtypeStruct((batch, 2, hk, ncp, LANES), F32),
        compiler_params=_params(("arbitrary", "arbitrary", "arbitrary")),
        name="nsa_compress",
    )(kvc, jnp.tile(pe, (1, 1, 2)).reshape(2, CMP_BLOCK, 1, LANES), w1p,
      jnp.tile(b1, (1, 2)).reshape(2, 1, 2 * hidden), w2p, jnp.tile(b2p, (1, 2)).reshape(2, 1, 2 * LANES))


def _tile_lanes(a, n):
    return jnp.concatenate([a] * n, axis=1)


def _lane(rows):
    return lax.broadcasted_iota(jnp.int32, (rows, LANES), 1)


def _query_heads(q_slab):
    qf = q_slab.astype(F32)
    heads = []
    for g in range(GROUP):
        x = qf[:, (g // 2) * LANES:(g // 2 + 1) * LANES]
        heads.append(pltpu.roll(x, HEAD_DIM, 1) if g % 2 else x)
    return heads


def _stack_queries(heads, extra=None):
    row = lax.broadcasted_iota(jnp.int32, (LANES, heads[0].shape[0]), 0)
    tail = jnp.where(row == NULL_LANE, MASKED, 0.0 if extra is None else extra)
    return jnp.concatenate([jnp.where(row < HEAD_DIM, x.T, tail).astype(BF16) for x in heads], axis=1)


def _build_kv_tiles(k_src, v_src, col, odd, block_lanes, k_ref, vt_ref, front_nulls=0):
    nulls = list(range(front_nulls)) if front_nulls else [k_ref.shape[0] - 1]
    nt, tq = k_ref.shape[0] - len(nulls), k_ref.shape[1]
    first = front_nulls
    lane = _lane(tq)
    row = lax.broadcasted_iota(jnp.int32, (tq, LANES), 0)
    ones_lane = jnp.where(lane == HEAD_DIM, 1.0, 0.0)
    for t in range(nt):
        kx = k_src[t * tq:(t + 1) * tq, col:col + LANES].astype(F32)
        if odd:
            kx = pltpu.roll(kx, HEAD_DIM, 1)
        if block_lanes:
            block = t * (tq // SEL_BLOCK) + row // SEL_BLOCK
            extra = jnp.where(lane == HEAD_DIM + block, 1.0, 0.0)
        else:
            extra = 0.0
        k_ref[first + t] = jnp.where(lane < HEAD_DIM, kx, extra).astype(BF16)
        vx = v_src[t * tq:(t + 1) * tq, col:col + LANES].astype(F32)
        if odd:
            vx = pltpu.roll(vx, HEAD_DIM, 1)
        vx = jnp.where(lane < HEAD_DIM, vx, ones_lane)
        vt_ref[first + t] = vx.T[0:ACC_ROWS].astype(BF16)
    for n in nulls:
        k_ref[n] = jnp.where(lane == NULL_LANE, 1.0, 0.0).astype(BF16)
        vt_ref[n] = jnp.zeros(vt_ref.shape[1:], BF16)


def _unstack_output(o_t, tq):
    halves = []
    for p in range(GROUP // 2):
        pair = jnp.concatenate([o_t[:, (2 * p) * tq:(2 * p + 1) * tq],
                                o_t[:, (2 * p + 1) * tq:(2 * p + 2) * tq]], axis=0)
        halves.append(pair.T)
    return jnp.concatenate(halves, axis=1)


def _band_bias(q_pos, k_pos, window):
    d = q_pos - k_pos
    return _tile_lanes(jnp.where((d >= 0) & (d < window), 0.0, MASKED), GROUP)


def _nsa_attn_kernel(q_ref, ksrc_ref, vsrc_ref, kwsrc_ref, vwsrc_ref, kcc_ref, vcc_ref, gate_ref, o_ref,
                     ks_ref, vs_ref, kw_ref, vw_ref, kc_ref, vc_ref, qs_ref, s_ref, p_ref, acc_ref, gt_ref,
                     *, n_cmp, n_sel, hps):
    hb = pl.program_id(1)
    qi = pl.program_id(2)
    tq = q_ref.shape[2]
    ncp = kc_ref.shape[1]
    nsp = -(-n_sel // 8) * 8

    @pl.when(qi == 0)
    def _():
        for hp in range(hps):
            col, odd = (hp // 2) * LANES, hp % 2 == 1
            _build_kv_tiles(ksrc_ref, vsrc_ref, col, odd, True, ks_ref.at[hp], vs_ref.at[hp])
            _build_kv_tiles(kwsrc_ref, vwsrc_ref, col, odd, False, kw_ref.at[hp], vw_ref.at[hp],
                            front_nulls=NSA_WINDOW // tq)
            kc_ref[hp] = kcc_ref[0, 0, hp].astype(BF16)
            vc_ref[hp] = vcc_ref[0, 0, hp].T[0:HEAD_DIM].astype(BF16)

    s0 = qi * tq
    q_pos = s0 + lax.broadcasted_iota(jnp.int32, (1, tq), 1)
    q_pos4 = _tile_lanes(q_pos, GROUP)
    k_pos = s0 + lax.broadcasted_iota(jnp.int32, (tq, 1), 0)
    causal = _tile_lanes(jnp.where(k_pos <= q_pos, 0.0, MASKED), GROUP)
    n_idx = lax.broadcasted_iota(jnp.int32, (ncp, 1), 0)
    valid = (n_idx * CMP_STRIDE + (CMP_BLOCK - 1) <= q_pos4) & (n_idx < n_cmp)
    j_blk = lax.broadcasted_iota(jnp.int32, (nsp, ncp), 0)
    n_blk = lax.broadcasted_iota(jnp.int32, (nsp, ncp), 1)
    overlap = (jnp.minimum(n_blk * CMP_STRIDE + CMP_BLOCK, j_blk * SEL_BLOCK + SEL_BLOCK)
               - jnp.maximum(n_blk * CMP_STRIDE, j_blk * SEL_BLOCK))
    overlap = jnp.where((n_blk < n_cmp) & (j_blk < n_sel), jnp.maximum(overlap, 0), 0)
    weight = (overlap.astype(F32) * (1.0 / CMP_STRIDE)).astype(BF16)
    j_row = lax.broadcasted_iota(jnp.int32, (nsp, tq), 0).astype(F32)
    cur = jnp.right_shift(s0 + lax.broadcasted_iota(jnp.int32, (nsp, tq), 1), 6).astype(F32)
    forced = (j_row == 0) | (j_row == cur) | (j_row == cur - 1)
    gt_ref[...] = gate_ref[...].T
    reach = NSA_WINDOW // tq + 1
    band = _band_bias(lax.broadcasted_iota(jnp.int32, (1, tq), 1),
                      lax.broadcasted_iota(jnp.int32, (reach * tq, 1), 0) - NSA_WINDOW, NSA_WINDOW)

    def sel_scores(hp, kt):
        s = _dot(ks_ref[hp, jnp.maximum(kt, 0)], qs_ref[hp])
        s_ref[hp] = s
        return jnp.max(s, axis=0, keepdims=True)

    def front(hp):
        heads = _query_heads(q_ref[0, hp])
        q_plain = _stack_queries(heads)

        sc = jnp.where(valid, _dot(kc_ref[hp], q_plain), NEG_INF)
        mc = jnp.max(sc, axis=0, keepdims=True)
        mc = jnp.where(mc == NEG_INF, 0.0, mc)
        ec = jnp.where(valid, jnp.exp2(sc - mc), 0.0)
        den = jnp.sum(ec, axis=0, keepdims=True)
        pc = (ec / jnp.where(den > 0, den, 1.0)).astype(BF16)
        o_cmp = _dot(vc_ref[hp], pc)

        imp4 = _dot(weight, pc)
        imp = imp4[:, 0:tq]
        for g in range(1, GROUP):
            imp = imp + imp4[:, g * tq:(g + 1) * tq]
        score = jnp.where(forced, jnp.inf, jnp.where(j_row <= cur, imp, NEG_INF))
        chosen = jnp.zeros((nsp, tq), F32)
        for _ in range(min(SEL_TOPK, n_sel)):
            best = jnp.max(score, axis=0, keepdims=True)
            idx = jnp.min(jnp.where(score == best, j_row, float(nsp)), axis=0, keepdims=True)
            pick = j_row == idx
            chosen = jnp.where(pick, 1.0, chosen)
            score = jnp.where(pick, NEG_INF, score)

        pieces = [jnp.zeros((HEAD_DIM, tq), F32), jnp.where(chosen > 0.5, 0.0, MASKED)]
        if LANES - HEAD_DIM - nsp > 0:
            pieces.append(jnp.zeros((LANES - HEAD_DIM - nsp, tq), F32))
        qs_ref[hp] = _stack_queries(heads, jnp.concatenate(pieces, axis=0))

        s = _dot(kw_ref[hp, pl.ds(qi, reach)].reshape(reach * tq, LANES), q_plain) + band
        p = jnp.exp2(s - jnp.max(s, axis=0, keepdims=True)).astype(BF16)
        acc_win = _dot(vw_ref[hp, qi], p[0:tq])
        for i in range(1, reach):
            acc_win = acc_win + _dot(vw_ref[hp, qi + i], p[i * tq:(i + 1) * tq])

        s = _dot(ks_ref[hp, qi], qs_ref[hp]) + causal
        m = jnp.max(s, axis=0, keepdims=True)
        p_ref[hp] = jnp.exp2(s - m).astype(BF16)
        acc_ref[hp] = jnp.zeros((ACC_ROWS, GROUP * tq), F32)
        return o_cmp, acc_win, (m, jnp.ones_like(m), sel_scores(hp, qi - 1))

    fronts = [front(hp) for hp in range(hps)]

    def sel_body(j, carry):
        out = ()
        for hp in range(hps):
            m, alpha, s_max = carry[3 * hp:3 * hp + 3]
            acc_ref[hp] = alpha * acc_ref[hp] + _dot(vs_ref[hp, qi - j], p_ref[hp])
            m_new = jnp.maximum(m, s_max)
            p_ref[hp] = jnp.exp2(s_ref[hp] - m_new).astype(BF16)
            out += (m_new, jnp.exp2(m - m_new), sel_scores(hp, qi - 2 - j))
        return out

    final = lax.fori_loop(0, qi, sel_body, tuple(x for f in fronts for x in f[2]))

    def normalised(a):
        return a[0:HEAD_DIM] / a[HEAD_DIM:HEAD_DIM + 1]

    for hp in range(hps):
        o_cmp, acc_win, _ = fronts[hp]
        alpha = final[3 * hp + 1]
        acc_sel = alpha * acc_ref[hp] + _dot(vs_ref[hp, 0], p_ref[hp])

        def gate_row(branch):
            first = ((hb * hps + hp) * GROUP) * 3 + branch
            return jnp.concatenate([gt_ref[pl.ds(first + 3 * g, 1), :] for g in range(GROUP)], axis=1)

        o_t = gate_row(0) * o_cmp + gate_row(1) * normalised(acc_sel) + gate_row(2) * normalised(acc_win)
        o_ref[0, hp] = _unstack_output(o_t, tq).astype(o_ref.dtype)


def _nsa_attention(q, kvb, kvcc, gate, batch, seq, n_cmp, n_sel):
    tq = min(seq, 256)
    nq = seq // tq
    ncp = kvcc.shape[3]
    hps = NSA_HEADS_PER_STEP
    assert HEAD_DIM + n_sel <= NULL_LANE and tq % SEL_BLOCK == 0 and hps % 2 == 0 and NSA_WINDOW % tq == 0
    kern = functools.partial(_nsa_attn_kernel, n_cmp=n_cmp, n_sel=n_sel, hps=hps)
    kv_lanes = hps // 2 * LANES
    blocks = NSA_KV_HEADS * HEAD_DIM // kv_lanes

    def kv_block(which):
        return pl.BlockSpec((seq, kv_lanes), lambda b, h, i: (b, which * blocks + h))

    def per_head(shape):
        return pltpu.VMEM((hps,) + shape, BF16)

    return pl.pallas_call(
        kern,
        grid=(batch, NSA_KV_HEADS // hps, nq),
        in_specs=[pl.BlockSpec((1, hps, tq, SLAB), lambda b, h, i: (b, h, i, 0)),
                  kv_block(0), kv_block(1), kv_block(2), kv_block(3),
                  pl.BlockSpec((1, 1, hps, ncp, LANES), lambda b, h, i: (b, 0, h, 0, 0)),
                  pl.BlockSpec((1, 1, hps, ncp, LANES), lambda b, h, i: (b, 1, h, 0, 0)),
                  pl.BlockSpec((tq, LANES), lambda b, h, i: (b * nq + i, 0))],
        out_specs=pl.BlockSpec((1, hps, tq, SLAB), lambda b, h, i: (b, h, i, 0)),
        out_shape=jax.ShapeDtypeStruct(q.shape, BF16),
        scratch_shapes=[per_head((nq + 1, tq, LANES)), per_head((nq + 1, ACC_ROWS, tq)),
                        per_head((nq + NSA_WINDOW // tq, tq, LANES)),
                        per_head((nq + NSA_WINDOW // tq, ACC_ROWS, tq)),
                        per_head((ncp, LANES)), per_head((HEAD_DIM, ncp)),
                        per_head((LANES, GROUP * tq)),
                        pltpu.VMEM((hps, tq, GROUP * tq), F32), per_head((tq, GROUP * tq)),
                        pltpu.VMEM((hps, ACC_ROWS, GROUP * tq), F32),
                        pltpu.VMEM((LANES, tq), F32)],
        compiler_params=_params(("arbitrary", "arbitrary", "arbitrary")),
        name="nsa_attention",
    )(q, kvb, kvb, kvb, kvb, kvcc, kvcc, gate)


def _swa_attn_kernel(sink_ref, q_ref, ksrc_ref, vsrc_ref, o_ref, k_ref, v_ref, s_ref, p_ref):
    n_slabs, seq = q_ref.shape[1], q_ref.shape[2]
    tq = s_ref.shape[2] // GROUP
    w = k_ref.shape[2]
    reach = tq // w + 1
    slabs_per_kv = n_slabs // SWA_KV_HEADS
    n_items = seq // tq * n_slabs

    for kvh in range(SWA_KV_HEADS):
        _build_kv_tiles(ksrc_ref, vsrc_ref, 0, kvh == 1, False, k_ref.at[kvh], v_ref.at[kvh], front_nulls=1)

    k_pos = lax.broadcasted_iota(jnp.int32, (reach * w, 1), 0) - w
    bias = _band_bias(lax.broadcasted_iota(jnp.int32, (1, tq), 1), k_pos, w)

    def sink_row(slab):
        return jnp.concatenate(
            [jnp.full((1, tq), sink_ref[slab * GROUP + g] * LOG2E, F32) for g in range(GROUP)], axis=1)

    def scores(t, way):
        qi, slab = t // n_slabs, t % n_slabs
        q_stack = _stack_queries(_query_heads(q_ref[0, slab, pl.ds(qi * tq, tq), :]))
        keys = k_ref[slab // slabs_per_kv, pl.ds(qi * (tq // w), reach)].reshape(reach * w, LANES)
        s = _dot(keys, q_stack) + bias
        s_ref[way] = s
        return jnp.maximum(jnp.max(s, axis=0, keepdims=True), sink_row(slab))

    def probabilities(way, m):
        p_ref[way] = jnp.exp2(s_ref[way] - m).astype(BF16)

    def values(t, way, m):
        qi, slab = t // n_slabs, t % n_slabs
        kvh, first = slab // slabs_per_kv, qi * (tq // w)
        acc = _dot(v_ref[kvh, first], p_ref[way, 0:w, :])
        for i in range(1, reach):
            acc = acc + _dot(v_ref[kvh, first + i], p_ref[way, i * w:(i + 1) * w, :])
        denom = acc[HEAD_DIM:HEAD_DIM + 1] + jnp.exp2(sink_row(slab) - m)
        o_ref[0, slab, pl.ds(qi * tq, tq), :] = _unstack_output(acc[0:HEAD_DIM] / denom, tq).astype(o_ref.dtype)

    ways = s_ref.shape[0]
    n_groups = n_items // ways

    def scores_of(g):
        return tuple(scores(g * ways + a, a) for a in range(ways))

    def probabilities_of(shifts):
        for a in range(ways):
            probabilities(a, shifts[a])

    def values_of(g, shifts):
        for a in range(ways):
            values(g * ways + a, a, shifts[a])

    shifts = scores_of(0)
    probabilities_of(shifts)

    def body(g, carry):
        pending_values, pending_probabilities = carry
        values_of(g, pending_values)
        probabilities_of(pending_probabilities)
        return pending_probabilities, scores_of(g + 2)

    pending_values, pending_probabilities = lax.fori_loop(0, n_groups - 2, body, (shifts, scores_of(1)))
    values_of(n_groups - 2, pending_values)
    probabilities_of(pending_probabilities)
    values_of(n_groups - 1, pending_probabilities)


def _swa_attention(q, kvb, sinks, batch, seq):
    n_slabs = q.shape[1]
    tq = min(seq, 256)
    w = SWA_WINDOW
    ways = 2
    assert SWA_KV_HEADS * HEAD_DIM == LANES and SWA_KV_HEADS == 2 and tq % w == 0 and seq % tq == 0
    assert seq // tq * n_slabs >= 2 * ways and n_slabs % ways == 0
    whole = pl.BlockSpec((1, n_slabs, seq, SLAB), lambda b: (b, 0, 0, 0))
    return pl.pallas_call(
        _swa_attn_kernel,
        grid=(batch,),
        in_specs=[pl.BlockSpec(memory_space=pltpu.SMEM), whole,
                  pl.BlockSpec((seq, LANES), lambda b: (b, 0)),
                  pl.BlockSpec((seq, LANES), lambda b: (b, 1))],
        out_specs=whole,
        out_shape=jax.ShapeDtypeStruct(q.shape, BF16),
        scratch_shapes=[pltpu.VMEM((SWA_KV_HEADS, seq // w + 1, w, LANES), BF16),
                        pltpu.VMEM((SWA_KV_HEADS, seq // w + 1, ACC_ROWS, w), BF16),
                        pltpu.VMEM((ways, (tq // w + 1) * w, GROUP * tq), F32),
                        pltpu.VMEM((ways, (tq // w + 1) * w, GROUP * tq), BF16)],
        compiler_params=_params(("arbitrary",)),
        name="swa_attention",
    )(sinks, q, kvb, kvb)


def _post_kernel(o_ref, x_ref, g_ref, gate1_ref, sh2_ref, sc2_ref, gate2_ref,
                 wo_ref, wup_ref, wdn_ref, out_ref):
    y = _dot(_load_slabs(o_ref), wo_ref[...])
    x1 = x_ref[...] + (1.0 + gate1_ref[0]) * _rms(y, g_ref[1])
    hb = (_rms(x1, g_ref[2]) * (1.0 + sc2_ref[0]) + sh2_ref[0]).astype(BF16)
    d = x1.shape[1]
    acc = jnp.zeros_like(x1)
    for c in range(wup_ref.shape[1] // d):
        a = jnp.maximum(_dot(hb, wup_ref[:, c * d:(c + 1) * d]), 0.0)
        acc = acc + _dot((a * a).astype(BF16), wdn_ref[c * d:(c + 1) * d, :])
    out_ref[...] = x1 + (1.0 + gate2_ref[0]) * _rms(acc, g_ref[3])


def _post_mixer(o, x2, mod_rows, norm_g4, layer, batch, seq, wo, wup, wdn):
    t, d = x2.shape
    tm = min(seq, 1024)
    nt = seq // tm
    row = lambda i: (i, 0)
    mod_row = lambda part: pl.BlockSpec((1, 1, d), lambda i: ((layer * batch + i // nt) * 6 + part, 0, 0))
    return pl.pallas_call(
        _post_kernel,
        grid=(t // tm,),
        in_specs=[pl.BlockSpec((1, o.shape[1], tm, SLAB), lambda i: (i // nt, 0, i % nt, 0)),
                  pl.BlockSpec((tm, d), row),
                  pl.BlockSpec((4, 1, d), lambda i: (layer, 0, 0)),
                  mod_row(2), mod_row(3), mod_row(4), mod_row(5),
                  _resident(wo.shape, lambda i: (0, 0)),
                  _resident(wup.shape, lambda i: (0, 0)),
                  _resident(wdn.shape, lambda i: (0, 0))],
        out_specs=pl.BlockSpec((tm, d), row),
        out_shape=jax.ShapeDtypeStruct((t, d), F32),
        compiler_params=_params(("arbitrary",)),
        name="post_mixer_mlp",
    )(o, x2, norm_g4, mod_rows, mod_rows, mod_rows, mod_rows, wo, wup, wdn)


def kernel(x, c, positions, ada_w, ada_b, norm_g, nsa_w_in, nsa_w_out, nsa_cmp_pe, nsa_phi_w1, nsa_phi_b1,
           nsa_phi_w2, nsa_phi_b2, swa_w_in, swa_w_out, swa_sinks, mlp_w_up, mlp_w_down):
    batch, seq, d = x.shape
    depth = ada_w.shape[0]
    assert seq % min(seq, 256) == 0 and seq % SEL_BLOCK == 0, "sequence must tile into query blocks"
    t = batch * seq
    n_cmp = (seq - CMP_BLOCK) // CMP_STRIDE + 1
    n_sel = seq // SEL_BLOCK
    ncp = -(-(seq // CMP_STRIDE) // LANES) * LANES

    cos_t, sin_t = _rope_tables(positions)
    mod = _modulation(c, ada_w, ada_b)
    mod_rows = mod.reshape(depth * batch * 6, 1, d)
    norm_rows = norm_g.reshape(depth * 4, 1, d)

    x2 = x.reshape(t, d)
    qd = d
    for i in range(depth):
        a = i // 2
        if i % 2 == 0:
            kd = NSA_KV_HEADS * HEAD_DIM
            w_in = nsa_w_in[a].astype(BF16)
            n_gate = w_in.shape[1] - qd - 6 * kd
            wg = jnp.pad(w_in[:, qd + 6 * kd:], ((0, 0), (0, LANES - n_gate)))
            q, kvc, kvb, gate = _pre_mixer("nsa", x2, mod_rows, norm_rows, i, batch, seq,
                                           (w_in[:, :qd], w_in[:, qd:qd + 6 * kd], wg), cos_t, sin_t)
            kvcc = _compress(kvc, batch, seq, nsa_cmp_pe[a], nsa_phi_w1[a], nsa_phi_b1[a],
                             nsa_phi_w2[a], nsa_phi_b2[a], ncp)
            o = _nsa_attention(q, kvb, kvcc, gate, batch, seq, n_cmp, n_sel)
            wo = nsa_w_out[a].astype(BF16)
        else:
            w_in = swa_w_in[a].astype(BF16)
            q, kvb = _pre_mixer("swa", x2, mod_rows, norm_rows, i, batch, seq,
                                (w_in[:, :qd], w_in[:, qd:]), cos_t, sin_t)
            o = _swa_attention(q, kvb, swa_sinks[a], batch, seq)
            wo = swa_w_out[a].astype(BF16)
        x2 = _post_mixer(o, x2, mod_rows, norm_rows, i, batch, seq, wo,
                         mlp_w_up[i].astype(BF16), mlp_w_down[i].astype(BF16))
    return x2.reshape(batch, seq, d)
```

```python
import functools

import jax
import jax.numpy as jnp
from jax import lax
from jax.experimental import pallas as pl
from jax.experimental.pallas import tpu as pltpu

F32 = jnp.float32
BF16 = jnp.bfloat16

HEAD_DIM = 64
ROPE_THETA = 10000.0
NORM_EPS = 1e-6
LOG2E = 1.4426950408889634
Q_SCALE = HEAD_DIM ** -0.5 * LOG2E
GROUP = 4
SLAB = GROUP * HEAD_DIM
LANES = 128
NSA_KV_HEADS = 4
NSA_WINDOW = 512
CMP_BLOCK = 32
CMP_STRIDE = 16
SEL_BLOCK = 64
SEL_TOPK = 8
SWA_KV_HEADS = 2
SWA_WINDOW = 128
VMEM_LIMIT = 56 * 1024 * 1024
NEG_INF = float("-inf")
MASKED = -1e30
ACC_ROWS = 80
NULL_LANE = LANES - 1
NSA_HEADS_PER_STEP = 4


def _dot(a, b):
    return jnp.dot(a, b, preferred_element_type=F32)


def _params(sem):
    return pltpu.CompilerParams(dimension_semantics=sem, vmem_limit_bytes=VMEM_LIMIT)


def _resident(shape, index_map):
    return pl.BlockSpec(shape, index_map, pipeline_mode=pl.Buffered(1))


def _rope_table_kernel(pos_ref, inv_ref, cos_ref, sin_ref):
    ang = pos_ref[...].astype(F32) * inv_ref[...]
    lane = lax.broadcasted_iota(jnp.int32, ang.shape, 1)
    first_half = (lane % HEAD_DIM) < HEAD_DIM // 2
    cos_ref[...] = jnp.cos(ang)
    s = jnp.sin(ang)
    sin_ref[...] = jnp.where(first_half, -s, s)


def _rope_tables(positions):
    t = positions.size
    tm = min(t, 2048)
    inv = 1.0 / (ROPE_THETA ** (jnp.arange(0, HEAD_DIM, 2, dtype=F32) / HEAD_DIM))
    inv_lanes = jnp.tile(inv, LANES // (HEAD_DIM // 2)).reshape(1, LANES)
    return pl.pallas_call(
        _rope_table_kernel,
        grid=(t // tm,),
        in_specs=[pl.BlockSpec((tm, 1), lambda i: (i, 0)),
                  pl.BlockSpec((1, LANES), lambda i: (0, 0))],
        out_specs=[pl.BlockSpec((tm, LANES), lambda i: (i, 0)),
                   pl.BlockSpec((tm, LANES), lambda i: (i, 0))],
        out_shape=[jax.ShapeDtypeStruct((t, LANES), F32)] * 2,
        compiler_params=_params(("arbitrary",)),
        name="rope_tables",
    )(positions.reshape(t, 1), inv_lanes)


def _rope(v, cos, sin_signed):
    lane = lax.broadcasted_iota(jnp.int32, (v.shape[0], LANES), 1)
    first_half = (lane % HEAD_DIM) < HEAD_DIM // 2
    outs = []
    for j in range(v.shape[1] // LANES):
        s = v[:, j * LANES:(j + 1) * LANES]
        partner = jnp.where(first_half,
                            pltpu.roll(s, LANES - HEAD_DIM // 2, 1),
                            pltpu.roll(s, HEAD_DIM // 2, 1))
        outs.append(s * cos + partner * sin_signed)
    return outs[0] if len(outs) == 1 else jnp.concatenate(outs, axis=1)


def _mod_kernel(c_ref, w_ref, b_ref, o_ref):
    c = c_ref[...]
    cond = (c * jax.nn.sigmoid(c)).astype(BF16)
    o_ref[0] = _dot(cond, w_ref[0].astype(BF16)) + b_ref[0]


def _modulation(c, ada_w, ada_b):
    depth, d, n = ada_w.shape
    b = c.shape[0]
    tn = 1024
    return pl.pallas_call(
        _mod_kernel,
        grid=(depth, n // tn),
        in_specs=[pl.BlockSpec((b, d), lambda l, j: (0, 0)),
                  pl.BlockSpec((1, d, tn), lambda l, j: (l, 0, j)),
                  pl.BlockSpec((1, 1, tn), lambda l, j: (l, 0, j))],
        out_specs=pl.BlockSpec((1, b, tn), lambda l, j: (l, 0, j)),
        out_shape=jax.ShapeDtypeStruct((depth, b, n), F32),
        compiler_params=_params(("arbitrary", "arbitrary")),
        name="adaln_modulation",
    )(c, ada_w, ada_b.reshape(depth, 1, n))


def _rms(x, g):
    ms = jnp.mean(x * x, axis=-1, keepdims=True)
    return x * lax.rsqrt(ms + NORM_EPS) * g


def _store_slabs(ref, x):
    for s in range(ref.shape[1]):
        ref[0, s] = x[:, s * SLAB:(s + 1) * SLAB].astype(ref.dtype)


def _load_slabs(ref):
    return jnp.concatenate([ref[0, s] for s in range(ref.shape[1])], axis=1)


def _nsa_pre_kernel(x_ref, g_ref, sh_ref, sc_ref, wq_ref, wkv_ref, wg_ref, cos_ref, sin_ref,
                    q_ref, kvc_ref, kvb_ref, gate_ref):
    h = _rms(x_ref[...], g_ref[0]) * (1.0 + sc_ref[0]) + sh_ref[0]
    hb = h.astype(BF16)
    cos = cos_ref[...]
    sin = sin_ref[...]
    _store_slabs(q_ref, _rope(_dot(hb, wq_ref[...]), cos, sin) * Q_SCALE)
    kv = _dot(hb, wkv_ref[...])
    kd = kvc_ref.shape[1] // 2
    kc = _rope(kv[:, 0:kd], cos, sin)
    kvc_ref[...] = jnp.concatenate([kc, kv[:, kd:2 * kd]], axis=1)
    ks = _rope(kv[:, 2 * kd:3 * kd], cos, sin)
    kw = _rope(kv[:, 4 * kd:5 * kd], cos, sin)
    kvb_ref[...] = jnp.concatenate(
        [ks, kv[:, 3 * kd:4 * kd], kw, kv[:, 5 * kd:6 * kd]], axis=1).astype(BF16)
    gate_ref[...] = jax.nn.sigmoid(_dot(hb, wg_ref[...]))


def _swa_pre_kernel(x_ref, g_ref, sh_ref, sc_ref, wq_ref, wkv_ref, cos_ref, sin_ref,
                    q_ref, kvb_ref):
    h = _rms(x_ref[...], g_ref[0]) * (1.0 + sc_ref[0]) + sh_ref[0]
    hb = h.astype(BF16)
    cos = cos_ref[...]
    sin = sin_ref[...]
    _store_slabs(q_ref, _rope(_dot(hb, wq_ref[...]), cos, sin) * Q_SCALE)
    kv = _dot(hb, wkv_ref[...])
    kd = kvb_ref.shape[1] // 2
    k = _rope(kv[:, 0:kd], cos, sin)
    kvb_ref[...] = jnp.concatenate([k, kv[:, kd:]], axis=1).astype(BF16)


def _pre_mixer(kind, x2, mod_rows, norm_rows, layer, batch, seq, weights, cos_t, sin_t):
    t, d = x2.shape
    tm = min(seq, 1024)
    nt = seq // tm
    row = lambda b, i: (b * nt + i, 0)
    mod_row = lambda part: pl.BlockSpec((1, 1, d), lambda b, i: ((layer * batch + b) * 6 + part, 0, 0))
    common_in = [pl.BlockSpec((tm, d), row),
                 pl.BlockSpec((1, 1, d), lambda b, i: (layer * 4 + 0, 0, 0)),
                 mod_row(0), mod_row(1)]
    tab = [pl.BlockSpec((tm, LANES), row), pl.BlockSpec((tm, LANES), row)]
    q_spec = pl.BlockSpec((1, d // SLAB, tm, SLAB), lambda b, i: (b, 0, i, 0))
    q_shape = jax.ShapeDtypeStruct((batch, d // SLAB, seq, SLAB), BF16)
    if kind == "nsa":
        wq, wkv, wg = weights
        kd = NSA_KV_HEADS * HEAD_DIM
        return pl.pallas_call(
            _nsa_pre_kernel,
            grid=(batch, nt),
            in_specs=common_in + [_resident(wq.shape, lambda b, i: (0, 0)),
                                  _resident(wkv.shape, lambda b, i: (0, 0)),
                                  _resident(wg.shape, lambda b, i: (0, 0))] + tab,
            out_specs=[q_spec,
                       pl.BlockSpec((tm, 2 * kd), row),
                       pl.BlockSpec((tm, 4 * kd), row),
                       pl.BlockSpec((tm, LANES), row)],
            out_shape=[q_shape,
                       jax.ShapeDtypeStruct((t, 2 * kd), F32),
                       jax.ShapeDtypeStruct((t, 4 * kd), BF16),
                       jax.ShapeDtypeStruct((t, LANES), F32)],
            compiler_params=_params(("arbitrary", "arbitrary")),
            name="nsa_pre",
        )(x2, norm_rows, mod_rows, mod_rows, wq, wkv, wg, cos_t, sin_t)
    wq, wkv = weights
    kd = SWA_KV_HEADS * HEAD_DIM
    return pl.pallas_call(
        _swa_pre_kernel,
        grid=(batch, nt),
        in_specs=common_in + [_resident(wq.shape, lambda b, i: (0, 0)),
                              _resident(wkv.shape, lambda b, i: (0, 0))] + tab,
        out_specs=[q_spec,
                   pl.BlockSpec((tm, 2 * kd), row)],
        out_shape=[q_shape,
                   jax.ShapeDtypeStruct((t, 2 * kd), BF16)],
        compiler_params=_params(("arbitrary", "arbitrary")),
        name="swa_pre",
    )(x2, norm_rows, mod_rows, mod_rows, wq, wkv, cos_t, sin_t)


def _compress_kernel(x_ref, pe_ref, w1_ref, b1_ref, w2_ref, b2_ref, o_ref):
    nc = x_ref.shape[0] // CMP_STRIDE
    ncp = o_ref.shape[3]

    def chunk_rows(half):
        return jnp.concatenate(
            [x_ref[pl.ds(l, nc, stride=CMP_STRIDE), :] + pe_ref[0, CMP_STRIDE * half + l]
             for l in range(CMP_STRIDE)], axis=1).astype(BF16)

    lo = _dot(chunk_rows(0), w1_ref[0, 0])
    hi = _dot(chunk_rows(1), w1_ref[0, 1])
    pre = lo + pltpu.roll(hi, nc - 1, 0) + b1_ref[0]
    hid = jax.nn.gelu(pre, approximate=True).astype(BF16)
    out = _dot(hid, w2_ref[0]) + b2_ref[0]
    if ncp > nc:
        out = jnp.concatenate([out, jnp.zeros((ncp - nc, 2 * LANES), F32)], axis=0)
    o_ref[0, 0, 0] = out[:, 0:LANES]
    o_ref[0, 0, 1] = out[:, LANES:2 * LANES]


def _two_heads(w):
    eye = jnp.eye(2, dtype=w.dtype)
    blk = w[..., None, :, None, :] * eye[:, None, :, None]
    return blk.reshape(w.shape[:-2] + (2 * w.shape[-2], 2 * w.shape[-1]))


def _compress(kvc, batch, seq, pe, w1, b1, w2, b2, ncp):
    hk = NSA_KV_HEADS
    slabs = hk * HEAD_DIM // LANES
    hidden = w1.shape[-1]
    pad = LANES - HEAD_DIM
    half = CMP_BLOCK // 2
    w1p = _two_heads(w1.astype(BF16).reshape(2, 2, half, HEAD_DIM, hidden))
    w1p = w1p.reshape(2, 2, half * LANES, 2 * hidden)
    w2p = _two_heads(jnp.pad(w2.astype(BF16), ((0, 0), (0, 0), (0, pad))))
    b2p = jnp.pad(b2, ((0, 0), (0, pad)))
    return pl.pallas_call(
        _compress_kernel,
        grid=(2, batch, slabs),
        in_specs=[pl.BlockSpec((seq, LANES), lambda j, b, s: (b, j * slabs + s)),
                  pl.BlockSpec((1, CMP_BLOCK, 1, LANES), lambda j, b, s: (j, 0, 0, 0)),
                  pl.BlockSpec((1, 2, half * LANES, 2 * hidden), lambda j, b, s: (j, 0, 0, 0)),
                  pl.BlockSpec((1, 1, 2 * hidden), lambda j, b, s: (j, 0, 0)),
                  pl.BlockSpec((1, 2 * hidden, 2 * LANES), lambda j, b, s: (j, 0, 0)),
                  pl.BlockSpec((1, 1, 2 * LANES), lambda j, b, s: (j, 0, 0))],
        out_specs=pl.BlockSpec((1, 1, 2, ncp, LANES), lambda j, b, s: (b, j, s, 0, 0)),
        out_shape=jax.ShapeDtypeStruct((batch, 2, hk, ncp, LANES), F32),
        compiler_params=_params(("arbitrary", "arbitrary", "arbitrary")),
        name="nsa_compress",
    )(kvc, jnp.tile(pe, (1, 1, 2)).reshape(2, CMP_BLOCK, 1, LANES), w1p,
      jnp.tile(b1, (1, 2)).reshape(2, 1, 2 * hidden), w2p, jnp.tile(b2p, (1, 2)).reshape(2, 1, 2 * LANES))


def _tile_lanes(a, n):
    return jnp.concatenate([a] * n, axis=1)


def _lane(rows):
    return lax.broadcasted_iota(jnp.int32, (rows, LANES), 1)


def _query_heads(q_slab):
    qf = q_slab.astype(F32)
    heads = []
    for g in range(GROUP):
        x = qf[:, (g // 2) * LANES:(g // 2 + 1) * LANES]
        heads.append(pltpu.roll(x, HEAD_DIM, 1) if g % 2 else x)
    return heads


def _stack_queries(heads, extra=None):
    row = lax.broadcasted_iota(jnp.int32, (LANES, heads[0].shape[0]), 0)
    tail = jnp.where(row == NULL_LANE, MASKED, 0.0 if extra is None else extra)
    return jnp.concatenate([jnp.where(row < HEAD_DIM, x.T, tail).astype(BF16) for x in heads], axis=1)


def _build_kv_tiles(k_src, v_src, col, odd, block_lanes, k_ref, vt_ref, front_nulls=0):
    nulls = list(range(front_nulls)) if front_nulls else [k_ref.shape[0] - 1]
    nt, tq = k_ref.shape[0] - len(nulls), k_ref.shape[1]
    first = front_nulls
    lane = _lane(tq)
    row = lax.broadcasted_iota(jnp.int32, (tq, LANES), 0)
    ones_lane = jnp.where(lane == HEAD_DIM, 1.0, 0.0)
    for t in range(nt):
        kx = k_src[t * tq:(t + 1) * tq, col:col + LANES].astype(F32)
        if odd:
            kx = pltpu.roll(kx, HEAD_DIM, 1)
        if block_lanes:
            block = t * (tq // SEL_BLOCK) + row // SEL_BLOCK
            extra = jnp.where(lane == HEAD_DIM + block, 1.0, 0.0)
        else:
            extra = 0.0
        k_ref[first + t] = jnp.where(lane < HEAD_DIM, kx, extra).astype(BF16)
        vx = v_src[t * tq:(t + 1) * tq, col:col + LANES].astype(F32)
        if odd:
            vx = pltpu.roll(vx, HEAD_DIM, 1)
        vx = jnp.where(lane < HEAD_DIM, vx, ones_lane)
        vt_ref[first + t] = vx.T[0:ACC_ROWS].astype(BF16)
    for n in nulls:
        k_ref[n] = jnp.where(lane == NULL_LANE, 1.0, 0.0).astype(BF16)
        vt_ref[n] = jnp.zeros(vt_ref.shape[1:], BF16)


def _unstack_output(o_t, tq):
    halves = []
    for p in range(GROUP // 2):
        pair = jnp.concatenate([o_t[:, (2 * p) * tq:(2 * p + 1) * tq],
                                o_t[:, (2 * p + 1) * tq:(2 * p + 2) * tq]], axis=0)
        halves.append(pair.T)
    return jnp.concatenate(halves, axis=1)


def _band_bias(q_pos, k_pos, window):
    d = q_pos - k_pos
    return _tile_lanes(jnp.where((d >= 0) & (d < window), 0.0, MASKED), GROUP)


def _nsa_attn_kernel(q_ref, ksrc_ref, vsrc_ref, kwsrc_ref, vwsrc_ref, kcc_ref, vcc_ref, gate_ref, o_ref,
                     ks_ref, vs_ref, kw_ref, vw_ref, kc_ref, vc_ref, qs_ref, s_ref, p_ref, acc_ref, gt_ref,
                     *, n_cmp, n_sel, hps):
    hb = pl.program_id(1)
    qi = pl.program_id(2)
    tq = q_ref.shape[2]
    ncp = kc_ref.shape[1]
    nsp = -(-n_sel // 8) * 8

    @pl.when(qi == 0)
    def _():
        for hp in range(hps):
            col, odd = (hp // 2) * LANES, hp % 2 == 1
            _build_kv_tiles(ksrc_ref, vsrc_ref, col, odd, True, ks_ref.at[hp], vs_ref.at[hp])
            _build_kv_tiles(kwsrc_ref, vwsrc_ref, col, odd, False, kw_ref.at[hp], vw_ref.at[hp],
                            front_nulls=NSA_WINDOW // tq)
            kc_ref[hp] = kcc_ref[0, 0, hp].astype(BF16)
            vc_ref[hp] = vcc_ref[0, 0, hp].T[0:HEAD_DIM].astype(BF16)

    s0 = qi * tq
    q_pos = s0 + lax.broadcasted_iota(jnp.int32, (1, tq), 1)
    q_pos4 = _tile_lanes(q_pos, GROUP)
    k_pos = s0 + lax.broadcasted_iota(jnp.int32, (tq, 1), 0)
    causal = _tile_lanes(jnp.where(k_pos <= q_pos, 0.0, MASKED), GROUP)
    n_idx = lax.broadcasted_iota(jnp.int32, (ncp, 1), 0)
    valid = (n_idx * CMP_STRIDE + (CMP_BLOCK - 1) <= q_pos4) & (n_idx < n_cmp)
    j_blk = lax.broadcasted_iota(jnp.int32, (nsp, ncp), 0)
    n_blk = lax.broadcasted_iota(jnp.int32, (nsp, ncp), 1)
    overlap = (jnp.minimum(n_blk * CMP_STRIDE + CMP_BLOCK, j_blk * SEL_BLOCK + SEL_BLOCK)
               - jnp.maximum(n_blk * CMP_STRIDE, j_blk * SEL_BLOCK))
    overlap = jnp.where((n_blk < n_cmp) & (j_blk < n_sel), jnp.maximum(overlap, 0), 0)
    weight = (overlap.astype(F32) * (1.0 / CMP_STRIDE)).astype(BF16)
    j_row = lax.broadcasted_iota(jnp.int32, (nsp, tq), 0).astype(F32)
    cur = jnp.right_shift(s0 + lax.broadcasted_iota(jnp.int32, (nsp, tq), 1), 6).astype(F32)
    forced = (j_row == 0) | (j_row == cur) | (j_row == cur - 1)
    gt_ref[...] = gate_ref[...].T
    reach = NSA_WINDOW // tq + 1
    band = _band_bias(lax.broadcasted_iota(jnp.int32, (1, tq), 1),
                      lax.broadcasted_iota(jnp.int32, (reach * tq, 1), 0) - NSA_WINDOW, NSA_WINDOW)

    def sel_scores(hp, kt):
        s = _dot(ks_ref[hp, jnp.maximum(kt, 0)], qs_ref[hp])
        s_ref[hp] = s
        return jnp.max(s, axis=0, keepdims=True)

    def front(hp):
        heads = _query_heads(q_ref[0, hp])
        q_plain = _stack_queries(heads)

        sc = jnp.where(valid, _dot(kc_ref[hp], q_plain), NEG_INF)
        mc = jnp.max(sc, axis=0, keepdims=True)
        mc = jnp.where(mc == NEG_INF, 0.0, mc)
        ec = jnp.where(valid, jnp.exp2(sc - mc), 0.0)
        den = jnp.sum(ec, axis=0, keepdims=True)
        pc = (ec / jnp.where(den > 0, den, 1.0)).astype(BF16)
        o_cmp = _dot(vc_ref[hp], pc)

        imp4 = _dot(weight, pc)
        imp = imp4[:, 0:tq]
        for g in range(1, GROUP):
            imp = imp + imp4[:, g * tq:(g + 1) * tq]
        score = jnp.where(forced, jnp.inf, jnp.where(j_row <= cur, imp, NEG_INF))
        chosen = jnp.zeros((nsp, tq), F32)
        for _ in range(min(SEL_TOPK, n_sel)):
            best = jnp.max(score, axis=0, keepdims=True)
            idx = jnp.min(jnp.where(score == best, j_row, float(nsp)), axis=0, keepdims=True)
            pick = j_row == idx
            chosen = jnp.where(pick, 1.0, chosen)
            score = jnp.where(pick, NEG_INF, score)

        pieces = [jnp.zeros((HEAD_DIM, tq), F32), jnp.where(chosen > 0.5, 0.0, MASKED)]
        if LANES - HEAD_DIM - nsp > 0:
            pieces.append(jnp.zeros((LANES - HEAD_DIM - nsp, tq), F32))
        qs_ref[hp] = _stack_queries(heads, jnp.concatenate(pieces, axis=0))

        s = _dot(kw_ref[hp, pl.ds(qi, reach)].reshape(reach * tq, LANES), q_plain) + band
        p = jnp.exp2(s - jnp.max(s, axis=0, keepdims=True)).astype(BF16)
        acc_win = _dot(vw_ref[hp, qi], p[0:tq])
        for i in range(1, reach):
            acc_win = acc_win + _dot(vw_ref[hp, qi + i], p[i * tq:(i + 1) * tq])

        s = _dot(ks_ref[hp, qi], qs_ref[hp]) + causal
        m = jnp.max(s, axis=0, keepdims=True)
        p_ref[hp] = jnp.exp2(s - m).astype(BF16)
        acc_ref[hp] = jnp.zeros((ACC_ROWS, GROUP * tq), F32)
        return o_cmp, acc_win, (m, jnp.ones_like(m), sel_scores(hp, qi - 1))

    fronts = [front(hp) for hp in range(hps)]

    def sel_body(j, carry):
        out = ()
        for hp in range(hps):
            m, alpha, s_max = carry[3 * hp:3 * hp + 3]
            acc_ref[hp] = alpha * acc_ref[hp] + _dot(vs_ref[hp, qi - j], p_ref[hp])
            m_new = jnp.maximum(m, s_max)
            p_ref[hp] = jnp.exp2(s_ref[hp] - m_new).astype(BF16)
            out += (m_new, jnp.exp2(m - m_new), sel_scores(hp, qi - 2 - j))
        return out

    final = lax.fori_loop(0, qi, sel_body, tuple(x for f in fronts for x in f[2]))

    def normalised(a):
        return a[0:HEAD_DIM] / a[HEAD_DIM:HEAD_DIM + 1]

    for hp in range(hps):
        o_cmp, acc_win, _ = fronts[hp]
        alpha = final[3 * hp + 1]
        acc_sel = alpha * acc_ref[hp] + _dot(vs_ref[hp, 0], p_ref[hp])

        def gate_row(branch):
            first = ((hb * hps + hp) * GROUP) * 3 + branch
            return jnp.concatenate([gt_ref[pl.ds(first + 3 * g, 1), :] for g in range(GROUP)], axis=1)

        o_t = gate_row(0) * o_cmp + gate_row(1) * normalised(acc_sel) + gate_row(2) * normalised(acc_win)
        o_ref[0, hp] = _unstack_output(o_t, tq).astype(o_ref.dtype)


def _nsa_attention(q, kvb, kvcc, gate, batch, seq, n_cmp, n_sel):
    tq = min(seq, 256)
    nq = seq // tq
    ncp = kvcc.shape[3]
    hps = NSA_HEADS_PER_STEP
    assert HEAD_DIM + n_sel <= NULL_LANE and tq % SEL_BLOCK == 0 and hps % 2 == 0 and NSA_WINDOW % tq == 0
    kern = functools.partial(_nsa_attn_kernel, n_cmp=n_cmp, n_sel=n_sel, hps=hps)
    kv_lanes = hps // 2 * LANES
    blocks = NSA_KV_HEADS * HEAD_DIM // kv_lanes

    def kv_block(which):
        return pl.BlockSpec((seq, kv_lanes), lambda b, h, i: (b, which * blocks + h))

    def per_head(shape):
        return pltpu.VMEM((hps,) + shape, BF16)

    return pl.pallas_call(
        kern,
        grid=(batch, NSA_KV_HEADS // hps, nq),
        in_specs=[pl.BlockSpec((1, hps, tq, SLAB), lambda b, h, i: (b, h, i, 0)),
                  kv_block(0), kv_block(1), kv_block(2), kv_block(3),
                  pl.BlockSpec((1, 1, hps, ncp, LANES), lambda b, h, i: (b, 0, h, 0, 0)),
                  pl.BlockSpec((1, 1, hps, ncp, LANES), lambda b, h, i: (b, 1, h, 0, 0)),
                  pl.BlockSpec((tq, LANES), lambda b, h, i: (b * nq + i, 0))],
        out_specs=pl.BlockSpec((1, hps, tq, SLAB), lambda b, h, i: (b, h, i, 0)),
        out_shape=jax.ShapeDtypeStruct(q.shape, BF16),
        scratch_shapes=[per_head((nq + 1, tq, LANES)), per_head((nq + 1, ACC_ROWS, tq)),
                        per_head((nq + NSA_WINDOW // tq, tq, LANES)),
                        per_head((nq + NSA_WINDOW // tq, ACC_ROWS, tq)),
                        per_head((ncp, LANES)), per_head((HEAD_DIM, ncp)),
                        per_head((LANES, GROUP * tq)),
                        pltpu.VMEM((hps, tq, GROUP * tq), F32), per_head((tq, GROUP * tq)),
                        pltpu.VMEM((hps, ACC_ROWS, GROUP * tq), F32),
                        pltpu.VMEM((LANES, tq), F32)],
        compiler_params=_params(("arbitrary", "arbitrary", "arbitrary")),
        name="nsa_attention",
    )(q, kvb, kvb, kvb, kvb, kvcc, kvcc, gate)


def _swa_attn_kernel(sink_ref, q_ref, ksrc_ref, vsrc_ref, o_ref, k_ref, v_ref, s_ref, p_ref):
    n_slabs, seq = q_ref.shape[1], q_ref.shape[2]
    tq = s_ref.shape[2] // GROUP
    w = k_ref.shape[2]
    reach = tq // w + 1
    slabs_per_kv = n_slabs // SWA_KV_HEADS
    n_items = seq // tq * n_slabs

    for kvh in range(SWA_KV_HEADS):
        _build_kv_tiles(ksrc_ref, vsrc_ref, 0, kvh == 1, False, k_ref.at[kvh], v_ref.at[kvh], front_nulls=1)

    k_pos = lax.broadcasted_iota(jnp.int32, (reach * w, 1), 0) - w
    bias = _band_bias(lax.broadcasted_iota(jnp.int32, (1, tq), 1), k_pos, w)

    def sink_row(slab):
        return jnp.concatenate(
            [jnp.full((1, tq), sink_ref[slab * GROUP + g] * LOG2E, F32) for g in range(GROUP)], axis=1)

    def scores(t, way):
        qi, slab = t // n_slabs, t % n_slabs
        q_stack = _stack_queries(_query_heads(q_ref[0, slab, pl.ds(qi * tq, tq), :]))
        keys = k_ref[slab // slabs_per_kv, pl.ds(qi * (tq // w), reach)].reshape(reach * w, LANES)
        s = _dot(keys, q_stack) + bias
        s_ref[way] = s
        return jnp.maximum(jnp.max(s, axis=0, keepdims=True), sink_row(slab))

    def probabilities(way, m):
        p_ref[way] = jnp.exp2(s_ref[way] - m).astype(BF16)

    def values(t, way, m):
        qi, slab = t // n_slabs, t % n_slabs
        kvh, first = slab // slabs_per_kv, qi * (tq // w)
        acc = _dot(v_ref[kvh, first], p_ref[way, 0:w, :])
        for i in range(1, reach):
            acc = acc + _dot(v_ref[kvh, first + i], p_ref[way, i * w:(i + 1) * w, :])
        denom = acc[HEAD_DIM:HEAD_DIM + 1] + jnp.exp2(sink_row(slab) - m)
        o_ref[0, slab, pl.ds(qi * tq, tq), :] = _unstack_output(acc[0:HEAD_DIM] / denom, tq).astype(o_ref.dtype)

    ways = s_ref.shape[0]
    n_groups = n_items // ways

    def scores_of(g):
        return tuple(scores(g * ways + a, a) for a in range(ways))

    def probabilities_of(shifts):
        for a in range(ways):
            probabilities(a, shifts[a])

    def values_of(g, shifts):
        for a in range(ways):
            values(g * ways + a, a, shifts[a])

    shifts = scores_of(0)
    probabilities_of(shifts)

    def body(g, carry):
        pending_values, pending_probabilities = carry
        values_of(g, pending_values)
        probabilities_of(pending_probabilities)
        return pending_probabilities, scores_of(g + 2)

    pending_values, pending_probabilities = lax.fori_loop(0, n_groups - 2, body, (shifts, scores_of(1)))
    values_of(n_groups - 2, pending_values)
    probabilities_of(pending_probabilities)
    values_of(n_groups - 1, pending_probabilities)


def _swa_attention(q, kvb, sinks, batch, seq):
    n_slabs = q.shape[1]
    tq = min(seq, SWA_WINDOW)
    w = SWA_WINDOW
    ways = 4
    assert SWA_KV_HEADS * HEAD_DIM == LANES and SWA_KV_HEADS == 2 and tq % w == 0 and seq % tq == 0
    assert seq // tq * n_slabs >= 2 * ways and n_slabs % ways == 0
    whole = pl.BlockSpec((1, n_slabs, seq, SLAB), lambda b: (b, 0, 0, 0))
    return pl.pallas_call(
        _swa_attn_kernel,
        grid=(batch,),
        in_specs=[pl.BlockSpec(memory_space=pltpu.SMEM), whole,
                  pl.BlockSpec((seq, LANES), lambda b: (b, 0)),
                  pl.BlockSpec((seq, LANES), lambda b: (b, 1))],
        out_specs=whole,
        out_shape=jax.ShapeDtypeStruct(q.shape, BF16),
        scratch_shapes=[pltpu.VMEM((SWA_KV_HEADS, seq // w + 1, w, LANES), BF16),
                        pltpu.VMEM((SWA_KV_HEADS, seq // w + 1, ACC_ROWS, w), BF16),
                        pltpu.VMEM((ways, (tq // w + 1) * w, GROUP * tq), F32),
                        pltpu.VMEM((ways, (tq // w + 1) * w, GROUP * tq), BF16)],
        compiler_params=_params(("arbitrary",)),
        name="swa_attention",
    )(sinks, q, kvb, kvb)


def _post_kernel(o_ref, x_ref, g_ref, gate1_ref, sh2_ref, sc2_ref, gate2_ref,
                 wo_ref, wup_ref, wdn_ref, out_ref):
    y = _dot(_load_slabs(o_ref), wo_ref[...])
    x1 = x_ref[...] + (1.0 + gate1_ref[0]) * _rms(y, g_ref[1])
    hb = (_rms(x1, g_ref[2]) * (1.0 + sc2_ref[0]) + sh2_ref[0]).astype(BF16)
    d = x1.shape[1]
    acc = jnp.zeros_like(x1)
    for c in range(wup_ref.shape[1] // d):
        a = jnp.maximum(_dot(hb, wup_ref[:, c * d:(c + 1) * d]), 0.0)
        acc = acc + _dot((a * a).astype(BF16), wdn_ref[c * d:(c + 1) * d, :])
    out_ref[...] = x1 + (1.0 + gate2_ref[0]) * _rms(acc, g_ref[3])


def _post_mixer(o, x2, mod_rows, norm_g4, layer, batch, seq, wo, wup, wdn):
    t, d = x2.shape
    tm = min(seq, 1024)
    nt = seq // tm
    row = lambda i: (i, 0)
    mod_row = lambda part: pl.BlockSpec((1, 1, d), lambda i: ((layer * batch + i // nt) * 6 + part, 0, 0))
    return pl.pallas_call(
        _post_kernel,
        grid=(t // tm,),
        in_specs=[pl.BlockSpec((1, o.shape[1], tm, SLAB), lambda i: (i // nt, 0, i % nt, 0)),
                  pl.BlockSpec((tm, d), row),
                  pl.BlockSpec((4, 1, d), lambda i: (layer, 0, 0)),
                  mod_row(2), mod_row(3), mod_row(4), mod_row(5),
                  _resident(wo.shape, lambda i: (0, 0)),
                  _resident(wup.shape, lambda i: (0, 0)),
                  _resident(wdn.shape, lambda i: (0, 0))],
        out_specs=pl.BlockSpec((tm, d), row),
        out_shape=jax.ShapeDtypeStruct((t, d), F32),
        compiler_params=_params(("arbitrary",)),
        name="post_mixer_mlp",
    )(o, x2, norm_g4, mod_rows, mod_rows, mod_rows, mod_rows, wo, wup, wdn)


def kernel(x, c, positions, ada_w, ada_b, norm_g, nsa_w_in, nsa_w_out, nsa_cmp_pe, nsa_phi_w1, nsa_phi_b1,
           nsa_phi_w2, nsa_phi_b2, swa_w_in, swa_w_out, swa_sinks, mlp_w_up, mlp_w_down):
    batch, seq, d = x.shape
    depth = ada_w.shape[0]
    assert seq % min(seq, 256) == 0 and seq % SEL_BLOCK == 0, "sequence must tile into query blocks"
    t = batch * seq
    n_cmp = (seq - CMP_BLOCK) // CMP_STRIDE + 1
    n_sel = seq // SEL_BLOCK
    ncp = -(-(seq // CMP_STRIDE) // LANES) * LANES

    cos_t, sin_t = _rope_tables(positions)
    mod = _modulation(c, ada_w, ada_b)
    mod_rows = mod.reshape(depth * batch * 6, 1, d)
    norm_rows = norm_g.reshape(depth * 4, 1, d)

    x2 = x.reshape(t, d)
    qd = d
    for i in range(depth):
        a = i // 2
        if i % 2 == 0:
            kd = NSA_KV_HEADS * HEAD_DIM
            w_in = nsa_w_in[a].astype(BF16)
            n_gate = w_in.shape[1] - qd - 6 * kd
            wg = jnp.pad(w_in[:, qd + 6 * kd:], ((0, 0), (0, LANES - n_gate)))
            q, kvc, kvb, gate = _pre_mixer("nsa", x2, mod_rows, norm_rows, i, batch, seq,
                                           (w_in[:, :qd], w_in[:, qd:qd + 6 * kd], wg), cos_t, sin_t)
            kvcc = _compress(kvc, batch, seq, nsa_cmp_pe[a], nsa_phi_w1[a], nsa_phi_b1[a],
                             nsa_phi_w2[a], nsa_phi_b2[a], ncp)
            o = _nsa_attention(q, kvb, kvcc, gate, batch, seq, n_cmp, n_sel)
            wo = nsa_w_out[a].astype(BF16)
        else:
            w_in = swa_w_in[a].astype(BF16)
            q, kvb = _pre_mixer("swa", x2, mod_rows, norm_rows, i, batch, seq,
                                (w_in[:, :qd], w_in[:, qd:]), cos_t, sin_t)
            o = _swa_attention(q, kvb, swa_sinks[a], batch, seq)
            wo = swa_w_out[a].astype(BF16)
        x2 = _post_mixer(o, x2, mod_rows, norm_rows, i, batch, seq, wo,
                         mlp_w_up[i].astype(BF16), mlp_w_down[i].astype(BF16))
    return x2.reshape(batch, seq, d)
```

```python
import functools

import jax
import jax.numpy as jnp
from jax import lax
from jax.experimental import pallas as pl
from jax.experimental.pallas import tpu as pltpu

F32 = jnp.float32
BF16 = jnp.bfloat16

HEAD_DIM = 64
ROPE_THETA = 10000.0
NORM_EPS = 1e-6
LOG2E = 1.4426950408889634
Q_SCALE = HEAD_DIM ** -0.5 * LOG2E
GROUP = 4
SLAB = GROUP * HEAD_DIM
LANES = 128
NSA_KV_HEADS = 4
NSA_WINDOW = 512
CMP_BLOCK = 32
CMP_STRIDE = 16
SEL_BLOCK = 64
SEL_TOPK = 8
SWA_KV_HEADS = 2
SWA_WINDOW = 128
VMEM_LIMIT = 56 * 1024 * 1024
NEG_INF = float("-inf")
MASKED = -1e30
ACC_ROWS = 80
NULL_LANE = LANES - 1
NSA_HEADS_PER_STEP = 4


def _dot(a, b):
    return jnp.dot(a, b, preferred_element_type=F32)


def _params(sem):
    return pltpu.CompilerParams(dimension_semantics=sem, vmem_limit_bytes=VMEM_LIMIT)


def _resident(shape, index_map):
    return pl.BlockSpec(shape, index_map, pipeline_mode=pl.Buffered(1))


def _rope_table_kernel(pos_ref, inv_ref, cos_ref, sin_ref):
    ang = pos_ref[...].astype(F32) * inv_ref[...]
    lane = lax.broadcasted_iota(jnp.int32, ang.shape, 1)
    first_half = (lane % HEAD_DIM) < HEAD_DIM // 2
    cos_ref[...] = jnp.cos(ang)
    s = jnp.sin(ang)
    sin_ref[...] = jnp.where(first_half, -s, s)


def _rope_tables(positions):
    t = positions.size
    tm = min(t, 2048)
    inv = 1.0 / (ROPE_THETA ** (jnp.arange(0, HEAD_DIM, 2, dtype=F32) / HEAD_DIM))
    inv_lanes = jnp.tile(inv, LANES // (HEAD_DIM // 2)).reshape(1, LANES)
    return pl.pallas_call(
        _rope_table_kernel,
        grid=(t // tm,),
        in_specs=[pl.BlockSpec((tm, 1), lambda i: (i, 0)),
                  pl.BlockSpec((1, LANES), lambda i: (0, 0))],
        out_specs=[pl.BlockSpec((tm, LANES), lambda i: (i, 0)),
                   pl.BlockSpec((tm, LANES), lambda i: (i, 0))],
        out_shape=[jax.ShapeDtypeStruct((t, LANES), F32)] * 2,
        compiler_params=_params(("arbitrary",)),
        name="rope_tables",
    )(positions.reshape(t, 1), inv_lanes)


def _rope(v, cos, sin_signed):
    lane = lax.broadcasted_iota(jnp.int32, (v.shape[0], LANES), 1)
    first_half = (lane % HEAD_DIM) < HEAD_DIM // 2
    outs = []
    for j in range(v.shape[1] // LANES):
        s = v[:, j * LANES:(j + 1) * LANES]
        partner = jnp.where(first_half,
                            pltpu.roll(s, LANES - HEAD_DIM // 2, 1),
                            pltpu.roll(s, HEAD_DIM // 2, 1))
        outs.append(s * cos + partner * sin_signed)
    return outs[0] if len(outs) == 1 else jnp.concatenate(outs, axis=1)


def _mod_kernel(c_ref, w_ref, b_ref, o_ref):
    c = c_ref[...]
    cond = (c * jax.nn.sigmoid(c)).astype(BF16)
    o_ref[0] = _dot(cond, w_ref[0].astype(BF16)) + b_ref[0]


def _modulation(c, ada_w, ada_b):
    depth, d, n = ada_w.shape
    b = c.shape[0]
    tn = 1024
    return pl.pallas_call(
        _mod_kernel,
        grid=(depth, n // tn),
        in_specs=[pl.BlockSpec((b, d), lambda l, j: (0, 0)),
                  pl.BlockSpec((1, d, tn), lambda l, j: (l, 0, j)),
                  pl.BlockSpec((1, 1, tn), lambda l, j: (l, 0, j))],
        out_specs=pl.BlockSpec((1, b, tn), lambda l, j: (l, 0, j)),
        out_shape=jax.ShapeDtypeStruct((depth, b, n), F32),
        compiler_params=_params(("arbitrary", "arbitrary")),
        name="adaln_modulation",
    )(c, ada_w, ada_b.reshape(depth, 1, n))


def _rms(x, g):
    ms = jnp.mean(x * x, axis=-1, keepdims=True)
    return x * lax.rsqrt(ms + NORM_EPS) * g


def _store_slabs(ref, x):
    for s in range(ref.shape[1]):
        ref[0, s] = x[:, s * SLAB:(s + 1) * SLAB].astype(ref.dtype)


def _load_slabs(ref):
    return jnp.concatenate([ref[0, s] for s in range(ref.shape[1])], axis=1)


def _nsa_pre_kernel(x_ref, g_ref, sh_ref, sc_ref, wq_ref, wkv0_ref, wkv1_ref, wkv2_ref, wg_ref,
                    cos_ref, sin_ref, q_ref, kvc_ref, kvb_ref, gate_ref):
    h = _rms(x_ref[...], g_ref[0]) * (1.0 + sc_ref[0]) + sh_ref[0]
    hb = h.astype(BF16)
    cos = cos_ref[...]
    sin = sin_ref[...]
    _store_slabs(q_ref, _rope(_dot(hb, wq_ref[0]), cos, sin) * Q_SCALE)
    kv = jnp.concatenate([_dot(hb, w[0]) for w in (wkv0_ref, wkv1_ref, wkv2_ref)],
                         axis=1)
    kd = kvc_ref.shape[1] // 2
    kc = _rope(kv[:, 0:kd], cos, sin)
    kvc_ref[...] = jnp.concatenate([kc, kv[:, kd:2 * kd]], axis=1)
    ks = _rope(kv[:, 2 * kd:3 * kd], cos, sin)
    kw = _rope(kv[:, 4 * kd:5 * kd], cos, sin)
    kvb_ref[...] = jnp.concatenate(
        [ks, kv[:, 3 * kd:4 * kd], kw, kv[:, 5 * kd:6 * kd]], axis=1).astype(BF16)
    gate_ref[...] = jax.nn.sigmoid(_dot(hb, wg_ref[...]))


def _swa_pre_kernel(x_ref, g_ref, sh_ref, sc_ref, wq_ref, wkv_ref, cos_ref, sin_ref,
                    q_ref, kvb_ref):
    h = _rms(x_ref[...], g_ref[0]) * (1.0 + sc_ref[0]) + sh_ref[0]
    hb = h.astype(BF16)
    cos = cos_ref[...]
    sin = sin_ref[...]
    _store_slabs(q_ref, _rope(_dot(hb, wq_ref[0]), cos, sin) * Q_SCALE)
    kv = _dot(hb, wkv_ref[0])
    kd = kvb_ref.shape[1] // 2
    k = _rope(kv[:, 0:kd], cos, sin)
    kvb_ref[...] = jnp.concatenate([k, kv[:, kd:]], axis=1).astype(BF16)


def _pre_mixer(kind, x2, mod_rows, norm_rows, layer, batch, seq, weights, cos_t, sin_t):
    t, d = x2.shape
    tm = min(seq, 1024)
    nt = seq // tm
    row = lambda b, i: (b * nt + i, 0)
    mod_row = lambda part: pl.BlockSpec((1, 1, d), lambda b, i: ((layer * batch + b) * 6 + part, 0, 0))
    common_in = [pl.BlockSpec((tm, d), row),
                 pl.BlockSpec((1, 1, d), lambda b, i: (layer * 4 + 0, 0, 0)),
                 mod_row(0), mod_row(1)]
    tab = [pl.BlockSpec((tm, LANES), row), pl.BlockSpec((tm, LANES), row)]
    q_spec = pl.BlockSpec((1, d // SLAB, tm, SLAB), lambda b, i: (b, 0, i, 0))
    q_shape = jax.ShapeDtypeStruct((batch, d // SLAB, seq, SLAB), BF16)
    w_in, which = weights[0], weights[1]

    def w_cols(width, col_block):
        return _resident((1, d, width), lambda b, i: (which, 0, col_block))

    if kind == "nsa":
        wg = weights[2]
        kd = NSA_KV_HEADS * HEAD_DIM
        return pl.pallas_call(
            _nsa_pre_kernel,
            grid=(batch, nt),
            in_specs=common_in + [w_cols(d, 0)] + [w_cols(2 * kd, d // (2 * kd) + j) for j in range(3)]
                     + [_resident(wg.shape, lambda b, i: (0, 0))] + tab,
            out_specs=[q_spec,
                       pl.BlockSpec((tm, 2 * kd), row),
                       pl.BlockSpec((tm, 4 * kd), row),
                       pl.BlockSpec((tm, LANES), row)],
            out_shape=[q_shape,
                       jax.ShapeDtypeStruct((t, 2 * kd), F32),
                       jax.ShapeDtypeStruct((t, 4 * kd), BF16),
                       jax.ShapeDtypeStruct((t, LANES), F32)],
            compiler_params=_params(("arbitrary", "arbitrary")),
            name="nsa_pre",
        )(x2, norm_rows, mod_rows, mod_rows, w_in, w_in, w_in, w_in, wg, cos_t, sin_t)
    kd = SWA_KV_HEADS * HEAD_DIM
    return pl.pallas_call(
        _swa_pre_kernel,
        grid=(batch, nt),
        in_specs=common_in + [w_cols(d, 0), w_cols(2 * kd, d // (2 * kd))] + tab,
        out_specs=[q_spec,
                   pl.BlockSpec((tm, 2 * kd), row)],
        out_shape=[q_shape,
                   jax.ShapeDtypeStruct((t, 2 * kd), BF16)],
        compiler_params=_params(("arbitrary", "arbitrary")),
        name="swa_pre",
    )(x2, norm_rows, mod_rows, mod_rows, w_in, w_in, cos_t, sin_t)


def _compress_kernel(x_ref, pe_ref, w1_ref, b1_ref, w2_ref, b2_ref, o_ref):
    nc = x_ref.shape[0] // CMP_STRIDE
    ncp = o_ref.shape[3]

    def chunk_rows(half):
        return jnp.concatenate(
            [x_ref[pl.ds(l, nc, stride=CMP_STRIDE), :] + pe_ref[0, CMP_STRIDE * half + l]
             for l in range(CMP_STRIDE)], axis=1).astype(BF16)

    lo = _dot(chunk_rows(0), w1_ref[0, 0])
    hi = _dot(chunk_rows(1), w1_ref[0, 1])
    pre = lo + pltpu.roll(hi, nc - 1, 0) + b1_ref[0]
    hid = jax.nn.gelu(pre, approximate=True).astype(BF16)
    out = _dot(hid, w2_ref[0]) + b2_ref[0]
    if ncp > nc:
        out = jnp.concatenate([out, jnp.zeros((ncp - nc, 2 * LANES), F32)], axis=0)
    o_ref[0, 0, 0] = out[:, 0:LANES]
    o_ref[0, 0, 1] = out[:, LANES:2 * LANES]


def _two_heads(w):
    eye = jnp.eye(2, dtype=w.dtype)
    blk = w[..., None, :, None, :] * eye[:, None, :, None]
    return blk.reshape(w.shape[:-2] + (2 * w.shape[-2], 2 * w.shape[-1]))


def _compress(kvc, batch, seq, pe, w1, b1, w2, b2, ncp):
    hk = NSA_KV_HEADS
    slabs = hk * HEAD_DIM // LANES
    hidden = w1.shape[-1]
    pad = LANES - HEAD_DIM
    half = CMP_BLOCK // 2
    w1p = _two_heads(w1.astype(BF16).reshape(2, 2, half, HEAD_DIM, hidden))
    w1p = w1p.reshape(2, 2, half * LANES, 2 * hidden)
    w2p = _two_heads(jnp.pad(w2.astype(BF16), ((0, 0), (0, 0), (0, pad))))
    b2p = jnp.pad(b2, ((0, 0), (0, pad)))
    return pl.pallas_call(
        _compress_kernel,
        grid=(2, batch, slabs),
        in_specs=[pl.BlockSpec((seq, LANES), lambda j, b, s: (b, j * slabs + s)),
                  pl.BlockSpec((1, CMP_BLOCK, 1, LANES), lambda j, b, s: (j, 0, 0, 0)),
                  pl.BlockSpec((1, 2, half * LANES, 2 * hidden), lambda j, b, s: (j, 0, 0, 0)),
                  pl.BlockSpec((1, 1, 2 * hidden), lambda j, b, s: (j, 0, 0)),
                  pl.BlockSpec((1, 2 * hidden, 2 * LANES), lambda j, b, s: (j, 0, 0)),
                  pl.BlockSpec((1, 1, 2 * LANES), lambda j, b, s: (j, 0, 0))],
        out_specs=pl.BlockSpec((1, 1, 2, ncp, LANES), lambda j, b, s: (b, j, s, 0, 0)),
        out_shape=jax.ShapeDtypeStruct((batch, 2, hk, ncp, LANES), F32),
        compiler_params=_params(("arbitrary", "arbitrary", "arbitrary")),
        name="nsa_compress",
    )(kvc, jnp.tile(pe, (1, 1, 2)).reshape(2, CMP_BLOCK, 1, LANES), w1p,
      jnp.tile(b1, (1, 2)).reshape(2, 1, 2 * hidden), w2p, jnp.tile(b2p, (1, 2)).reshape(2, 1, 2 * LANES))


def _tile_lanes(a, n):
    return jnp.concatenate([a] * n, axis=1)


def _lane(rows):
    return lax.broadcasted_iota(jnp.int32, (rows, LANES), 1)


def _query_heads(q_slab):
    qf = q_slab.astype(F32)
    heads = []
    for g in range(GROUP):
        x = qf[:, (g // 2) * LANES:(g // 2 + 1) * LANES]
        heads.append(pltpu.roll(x, HEAD_DIM, 1) if g % 2 else x)
    return heads


def _stack_queries(heads, extra=None):
    row = lax.broadcasted_iota(jnp.int32, (LANES, heads[0].shape[0]), 0)
    tail = jnp.where(row == NULL_LANE, MASKED, 0.0 if extra is None else extra)
    return jnp.concatenate([jnp.where(row < HEAD_DIM, x.T, tail).astype(BF16) for x in heads], axis=1)


def _build_kv_tiles(k_src, v_src, col, odd, block_lanes, k_ref, vt_ref, front_nulls=0):
    nulls = list(range(front_nulls)) if front_nulls else [k_ref.shape[0] - 1]
    nt, tq = k_ref.shape[0] - len(nulls), k_ref.shape[1]
    first = front_nulls
    lane = _lane(tq)
    row = lax.broadcasted_iota(jnp.int32, (tq, LANES), 0)
    ones_lane = jnp.where(lane == HEAD_DIM, 1.0, 0.0)
    for t in range(nt):
        kx = k_src[t * tq:(t + 1) * tq, col:col + LANES].astype(F32)
        if odd:
            kx = pltpu.roll(kx, HEAD_DIM, 1)
        if block_lanes:
            block = t * (tq // SEL_BLOCK) + row // SEL_BLOCK
            extra = jnp.where(lane == HEAD_DIM + block, 1.0, 0.0)
        else:
            extra = 0.0
        k_ref[first + t] = jnp.where(lane < HEAD_DIM, kx, extra).astype(BF16)
        vx = v_src[t * tq:(t + 1) * tq, col:col + LANES].astype(F32)
        if odd:
            vx = pltpu.roll(vx, HEAD_DIM, 1)
        vx = jnp.where(lane < HEAD_DIM, vx, ones_lane)
        vt_ref[first + t] = vx.T[0:ACC_ROWS].astype(BF16)
    for n in nulls:
        k_ref[n] = jnp.where(lane == NULL_LANE, 1.0, 0.0).astype(BF16)
        vt_ref[n] = jnp.zeros(vt_ref.shape[1:], BF16)


def _unstack_output(o_t, tq):
    halves = []
    for p in range(GROUP // 2):
        pair = jnp.concatenate([o_t[:, (2 * p) * tq:(2 * p + 1) * tq],
                                o_t[:, (2 * p + 1) * tq:(2 * p + 2) * tq]], axis=0)
        halves.append(pair.T)
    return jnp.concatenate(halves, axis=1)


def _band_bias(q_pos, k_pos, window):
    d = q_pos - k_pos
    return _tile_lanes(jnp.where((d >= 0) & (d < window), 0.0, MASKED), GROUP)


def _nsa_attn_kernel(q_ref, ksrc_ref, vsrc_ref, kwsrc_ref, vwsrc_ref, kcc_ref, vcc_ref, gate_ref, o_ref,
                     ks_ref, vs_ref, kw_ref, vw_ref, kc_ref, vc_ref, qs_ref, s_ref, p_ref, acc_ref, gt_ref,
                     *, n_cmp, n_sel, hps):
    hb = pl.program_id(1)
    qi = pl.program_id(2)
    tq = q_ref.shape[2]
    ncp = kc_ref.shape[1]
    nsp = -(-n_sel // 8) * 8

    @pl.when(qi == 0)
    def _():
        for hp in range(hps):
            col, odd = (hp // 2) * LANES, hp % 2 == 1
            _build_kv_tiles(ksrc_ref, vsrc_ref, col, odd, True, ks_ref.at[hp], vs_ref.at[hp])
            _build_kv_tiles(kwsrc_ref, vwsrc_ref, col, odd, False, kw_ref.at[hp], vw_ref.at[hp],
                            front_nulls=NSA_WINDOW // tq)
            kc_ref[hp] = kcc_ref[0, 0, hp].astype(BF16)
            vc_ref[hp] = vcc_ref[0, 0, hp].T[0:HEAD_DIM].astype(BF16)

    s0 = qi * tq
    q_pos = s0 + lax.broadcasted_iota(jnp.int32, (1, tq), 1)
    q_pos4 = _tile_lanes(q_pos, GROUP)
    k_pos = s0 + lax.broadcasted_iota(jnp.int32, (tq, 1), 0)
    causal = _tile_lanes(jnp.where(k_pos <= q_pos, 0.0, MASKED), GROUP)
    n_idx = lax.broadcasted_iota(jnp.int32, (ncp, 1), 0)
    valid = (n_idx * CMP_STRIDE + (CMP_BLOCK - 1) <= q_pos4) & (n_idx < n_cmp)
    j_blk = lax.broadcasted_iota(jnp.int32, (nsp, ncp), 0)
    n_blk = lax.broadcasted_iota(jnp.int32, (nsp, ncp), 1)
    overlap = (jnp.minimum(n_blk * CMP_STRIDE + CMP_BLOCK, j_blk * SEL_BLOCK + SEL_BLOCK)
               - jnp.maximum(n_blk * CMP_STRIDE, j_blk * SEL_BLOCK))
    overlap = jnp.where((n_blk < n_cmp) & (j_blk < n_sel), jnp.maximum(overlap, 0), 0)
    weight = (overlap.astype(F32) * (1.0 / CMP_STRIDE)).astype(BF16)
    j_row = lax.broadcasted_iota(jnp.int32, (nsp, tq), 0).astype(F32)
    cur = jnp.right_shift(s0 + lax.broadcasted_iota(jnp.int32, (nsp, tq), 1), 6).astype(F32)
    forced = (j_row == 0) | (j_row == cur) | (j_row == cur - 1)
    gt_ref[...] = gate_ref[...].T
    reach = NSA_WINDOW // tq + 1
    band = _band_bias(lax.broadcasted_iota(jnp.int32, (1, tq), 1),
                      lax.broadcasted_iota(jnp.int32, (reach * tq, 1), 0) - NSA_WINDOW, NSA_WINDOW)

    def sel_scores(hp, kt):
        s = _dot(ks_ref[hp, jnp.maximum(kt, 0)], qs_ref[hp])
        s_ref[hp] = s
        return jnp.max(s, axis=0, keepdims=True)

    def front(hp):
        heads = _query_heads(q_ref[0, hp])
        q_plain = _stack_queries(heads)

        sc = jnp.where(valid, _dot(kc_ref[hp], q_plain), NEG_INF)
        mc = jnp.max(sc, axis=0, keepdims=True)
        mc = jnp.where(mc == NEG_INF, 0.0, mc)
        ec = jnp.where(valid, jnp.exp2(sc - mc), 0.0)
        den = jnp.sum(ec, axis=0, keepdims=True)
        pc = (ec / jnp.where(den > 0, den, 1.0)).astype(BF16)
        o_cmp = _dot(vc_ref[hp], pc)

        imp4 = _dot(weight, pc)
        imp = imp4[:, 0:tq]
        for g in range(1, GROUP):
            imp = imp + imp4[:, g * tq:(g + 1) * tq]
        score = jnp.where(forced, jnp.inf, jnp.where(j_row <= cur, imp, NEG_INF))
        chosen = jnp.zeros((nsp, tq), F32)
        for _ in range(min(SEL_TOPK, n_sel)):
            best = jnp.max(score, axis=0, keepdims=True)
            idx = jnp.min(jnp.where(score == best, j_row, float(nsp)), axis=0, keepdims=True)
            pick = j_row == idx
            chosen = jnp.where(pick, 1.0, chosen)
            score = jnp.where(pick, NEG_INF, score)

        pieces = [jnp.zeros((HEAD_DIM, tq), F32), jnp.where(chosen > 0.5, 0.0, MASKED)]
        if LANES - HEAD_DIM - nsp > 0:
            pieces.append(jnp.zeros((LANES - HEAD_DIM - nsp, tq), F32))
        qs_ref[hp] = _stack_queries(heads, jnp.concatenate(pieces, axis=0))

        s = _dot(kw_ref[hp, pl.ds(qi, reach)].reshape(reach * tq, LANES), q_plain) + band
        p = jnp.exp2(s - jnp.max(s, axis=0, keepdims=True)).astype(BF16)
        acc_win = _dot(vw_ref[hp, qi], p[0:tq])
        for i in range(1, reach):
            acc_win = acc_win + _dot(vw_ref[hp, qi + i], p[i * tq:(i + 1) * tq])

        s = _dot(ks_ref[hp, qi], qs_ref[hp]) + causal
        m = jnp.max(s, axis=0, keepdims=True)
        p_ref[hp] = jnp.exp2(s - m).astype(BF16)
        acc_ref[hp] = jnp.zeros((ACC_ROWS, GROUP * tq), F32)
        return o_cmp, acc_win, (m, jnp.ones_like(m), sel_scores(hp, qi - 1))

    fronts = [front(hp) for hp in range(hps)]

    def sel_body(j, carry):
        out = ()
        for hp in range(hps):
            m, alpha, s_max = carry[3 * hp:3 * hp + 3]
            acc_ref[hp] = alpha * acc_ref[hp] + _dot(vs_ref[hp, qi - j], p_ref[hp])
            m_new = jnp.maximum(m, s_max)
            p_ref[hp] = jnp.exp2(s_ref[hp] - m_new).astype(BF16)
            out += (m_new, jnp.exp2(m - m_new), sel_scores(hp, qi - 2 - j))
        return out

    final = lax.fori_loop(0, qi, sel_body, tuple(x for f in fronts for x in f[2]))

    def normalised(a):
        return a[0:HEAD_DIM] / a[HEAD_DIM:HEAD_DIM + 1]

    for hp in range(hps):
        o_cmp, acc_win, _ = fronts[hp]
        alpha = final[3 * hp + 1]
        acc_sel = alpha * acc_ref[hp] + _dot(vs_ref[hp, 0], p_ref[hp])

        def gate_row(branch):
            first = ((hb * hps + hp) * GROUP) * 3 + branch
            return jnp.concatenate([gt_ref[pl.ds(first + 3 * g, 1), :] for g in range(GROUP)], axis=1)

        o_t = gate_row(0) * o_cmp + gate_row(1) * normalised(acc_sel) + gate_row(2) * normalised(acc_win)
        o_ref[0, hp] = _unstack_output(o_t, tq).astype(o_ref.dtype)


def _nsa_attention(q, kvb, kvcc, gate, batch, seq, n_cmp, n_sel):
    tq = min(seq, 256)
    nq = seq // tq
    ncp = kvcc.shape[3]
    hps = NSA_HEADS_PER_STEP
    assert HEAD_DIM + n_sel <= NULL_LANE and tq % SEL_BLOCK == 0 and hps % 2 == 0 and NSA_WINDOW % tq == 0
    kern = functools.partial(_nsa_attn_kernel, n_cmp=n_cmp, n_sel=n_sel, hps=hps)
    kv_lanes = hps // 2 * LANES
    blocks = NSA_KV_HEADS * HEAD_DIM // kv_lanes

    def kv_block(which):
        return pl.BlockSpec((seq, kv_lanes), lambda b, h, i: (b, which * blocks + h))

    def per_head(shape):
        return pltpu.VMEM((hps,) + shape, BF16)

    return pl.pallas_call(
        kern,
        grid=(batch, NSA_KV_HEADS // hps, nq),
        in_specs=[pl.BlockSpec((1, hps, tq, SLAB), lambda b, h, i: (b, h, i, 0)),
                  kv_block(0), kv_block(1), kv_block(2), kv_block(3),
                  pl.BlockSpec((1, 1, hps, ncp, LANES), lambda b, h, i: (b, 0, h, 0, 0)),
                  pl.BlockSpec((1, 1, hps, ncp, LANES), lambda b, h, i: (b, 1, h, 0, 0)),
                  pl.BlockSpec((tq, LANES), lambda b, h, i: (b * nq + i, 0))],
        out_specs=pl.BlockSpec((1, hps, tq, SLAB), lambda b, h, i: (b, h, i, 0)),
        out_shape=jax.ShapeDtypeStruct(q.shape, BF16),
        scratch_shapes=[per_head((nq + 1, tq, LANES)), per_head((nq + 1, ACC_ROWS, tq)),
                        per_head((nq + NSA_WINDOW // tq, tq, LANES)),
                        per_head((nq + NSA_WINDOW // tq, ACC_ROWS, tq)),
                        per_head((ncp, LANES)), per_head((HEAD_DIM, ncp)),
                        per_head((LANES, GROUP * tq)),
                        pltpu.VMEM((hps, tq, GROUP * tq), F32), per_head((tq, GROUP * tq)),
                        pltpu.VMEM((hps, ACC_ROWS, GROUP * tq), F32),
                        pltpu.VMEM((LANES, tq), F32)],
        compiler_params=_params(("arbitrary", "arbitrary", "arbitrary")),
        name="nsa_attention",
    )(q, kvb, kvb, kvb, kvb, kvcc, kvcc, gate)


def _swa_attn_kernel(sink_ref, q_ref, ksrc_ref, vsrc_ref, o_ref, k_ref, v_ref, s_ref, p_ref):
    n_slabs, seq = q_ref.shape[1], q_ref.shape[2]
    tq = s_ref.shape[2] // GROUP
    w = k_ref.shape[2]
    reach = tq // w + 1
    slabs_per_kv = n_slabs // SWA_KV_HEADS
    n_items = seq // tq * n_slabs

    for kvh in range(SWA_KV_HEADS):
        _build_kv_tiles(ksrc_ref, vsrc_ref, 0, kvh == 1, False, k_ref.at[kvh], v_ref.at[kvh], front_nulls=1)

    k_pos = lax.broadcasted_iota(jnp.int32, (reach * w, 1), 0) - w
    bias = _band_bias(lax.broadcasted_iota(jnp.int32, (1, tq), 1), k_pos, w)

    def sink_row(slab):
        return jnp.concatenate(
            [jnp.full((1, tq), sink_ref[slab * GROUP + g] * LOG2E, F32) for g in range(GROUP)], axis=1)

    def scores(t, way):
        qi, slab = t // n_slabs, t % n_slabs
        q_stack = _stack_queries(_query_heads(q_ref[0, slab, pl.ds(qi * tq, tq), :]))
        keys = k_ref[slab // slabs_per_kv, pl.ds(qi * (tq // w), reach)].reshape(reach * w, LANES)
        s = _dot(keys, q_stack) + bias
        s_ref[way] = s
        return jnp.maximum(jnp.max(s, axis=0, keepdims=True), sink_row(slab))

    def probabilities(way, m):
        p_ref[way] = jnp.exp2(s_ref[way] - m).astype(BF16)

    def values(t, way, m):
        qi, slab = t // n_slabs, t % n_slabs
        kvh, first = slab // slabs_per_kv, qi * (tq // w)
        acc = _dot(v_ref[kvh, first], p_ref[way, 0:w, :])
        for i in range(1, reach):
            acc = acc + _dot(v_ref[kvh, first + i], p_ref[way, i * w:(i + 1) * w, :])
        denom = acc[HEAD_DIM:HEAD_DIM + 1] + jnp.exp2(sink_row(slab) - m)
        o_ref[0, slab, pl.ds(qi * tq, tq), :] = _unstack_output(acc[0:HEAD_DIM] / denom, tq).astype(o_ref.dtype)

    ways = s_ref.shape[0]
    n_groups = n_items // ways

    def scores_of(g):
        return tuple(scores(g * ways + a, a) for a in range(ways))

    def probabilities_of(shifts):
        for a in range(ways):
            probabilities(a, shifts[a])

    def values_of(g, shifts):
        for a in range(ways):
            values(g * ways + a, a, shifts[a])

    shifts = scores_of(0)
    probabilities_of(shifts)

    def body(g, carry):
        pending_values, pending_probabilities = carry
        values_of(g, pending_values)
        probabilities_of(pending_probabilities)
        return pending_probabilities, scores_of(g + 2)

    pending_values, pending_probabilities = lax.fori_loop(0, n_groups - 2, body, (shifts, scores_of(1)))
    values_of(n_groups - 2, pending_values)
    probabilities_of(pending_probabilities)
    values_of(n_groups - 1, pending_probabilities)


def _swa_attention(q, kvb, sinks, batch, seq):
    n_slabs = q.shape[1]
    tq = min(seq, SWA_WINDOW)
    w = SWA_WINDOW
    ways = 4
    assert SWA_KV_HEADS * HEAD_DIM == LANES and SWA_KV_HEADS == 2 and tq % w == 0 and seq % tq == 0
    assert seq // tq * n_slabs >= 2 * ways and n_slabs % ways == 0
    whole = pl.BlockSpec((1, n_slabs, seq, SLAB), lambda b: (b, 0, 0, 0))
    return pl.pallas_call(
        _swa_attn_kernel,
        grid=(batch,),
        in_specs=[pl.BlockSpec(memory_space=pltpu.SMEM), whole,
                  pl.BlockSpec((seq, LANES), lambda b: (b, 0)),
                  pl.BlockSpec((seq, LANES), lambda b: (b, 1))],
        out_specs=whole,
        out_shape=jax.ShapeDtypeStruct(q.shape, BF16),
        scratch_shapes=[pltpu.VMEM((SWA_KV_HEADS, seq // w + 1, w, LANES), BF16),
                        pltpu.VMEM((SWA_KV_HEADS, seq // w + 1, ACC_ROWS, w), BF16),
                        pltpu.VMEM((ways, (tq // w + 1) * w, GROUP * tq), F32),
                        pltpu.VMEM((ways, (tq // w + 1) * w, GROUP * tq), BF16)],
        compiler_params=_params(("arbitrary",)),
        name="swa_attention",
    )(sinks, q, kvb, kvb)


def _post_kernel(o_ref, x_ref, g_ref, gate1_ref, sh2_ref, sc2_ref, gate2_ref,
                 wo_ref, wup_ref, wdn_ref, out_ref):
    y = _dot(_load_slabs(o_ref), wo_ref[0])
    x1 = x_ref[...] + (1.0 + gate1_ref[0]) * _rms(y, g_ref[1])
    hb = (_rms(x1, g_ref[2]) * (1.0 + sc2_ref[0]) + sh2_ref[0]).astype(BF16)
    d = x1.shape[1]
    acc = jnp.zeros_like(x1)
    for c in range(wup_ref.shape[2] // d):
        a = jnp.maximum(_dot(hb, wup_ref[0, :, c * d:(c + 1) * d]), 0.0)
        acc = acc + _dot((a * a).astype(BF16), wdn_ref[0, c * d:(c + 1) * d, :])
    out_ref[...] = x1 + (1.0 + gate2_ref[0]) * _rms(acc, g_ref[3])


def _post_mixer(o, x2, mod_rows, norm_g4, layer, batch, seq, wo, which, wup, wdn):
    t, d = x2.shape
    tm = min(seq, 1024)
    nt = seq // tm
    row = lambda i: (i, 0)
    mod_row = lambda part: pl.BlockSpec((1, 1, d), lambda i: ((layer * batch + i // nt) * 6 + part, 0, 0))
    return pl.pallas_call(
        _post_kernel,
        grid=(t // tm,),
        in_specs=[pl.BlockSpec((1, o.shape[1], tm, SLAB), lambda i: (i // nt, 0, i % nt, 0)),
                  pl.BlockSpec((tm, d), row),
                  pl.BlockSpec((4, 1, d), lambda i: (layer, 0, 0)),
                  mod_row(2), mod_row(3), mod_row(4), mod_row(5),
                  _resident((1,) + wo.shape[1:], lambda i: (which, 0, 0)),
                  _resident((1,) + wup.shape[1:], lambda i: (layer, 0, 0)),
                  _resident((1,) + wdn.shape[1:], lambda i: (layer, 0, 0))],
        out_specs=pl.BlockSpec((tm, d), row),
        out_shape=jax.ShapeDtypeStruct((t, d), F32),
        compiler_params=_params(("arbitrary",)),
        name="post_mixer_mlp",
    )(o, x2, norm_g4, mod_rows, mod_rows, mod_rows, mod_rows, wo, wup, wdn)


def kernel(x, c, positions, ada_w, ada_b, norm_g, nsa_w_in, nsa_w_out, nsa_cmp_pe, nsa_phi_w1, nsa_phi_b1,
           nsa_phi_w2, nsa_phi_b2, swa_w_in, swa_w_out, swa_sinks, mlp_w_up, mlp_w_down):
    batch, seq, d = x.shape
    depth = ada_w.shape[0]
    assert seq % min(seq, 256) == 0 and seq % SEL_BLOCK == 0, "sequence must tile into query blocks"
    t = batch * seq
    n_cmp = (seq - CMP_BLOCK) // CMP_STRIDE + 1
    n_sel = seq // SEL_BLOCK
    ncp = -(-(seq // CMP_STRIDE) // LANES) * LANES

    cos_t, sin_t = _rope_tables(positions)
    mod = _modulation(c, ada_w, ada_b)
    mod_rows = mod.reshape(depth * batch * 6, 1, d)
    norm_rows = norm_g.reshape(depth * 4, 1, d)

    x2 = x.reshape(t, d)
    qd = d
    nsa_in, nsa_out = nsa_w_in.astype(BF16), nsa_w_out.astype(BF16)
    swa_in, swa_out = swa_w_in.astype(BF16), swa_w_out.astype(BF16)
    w_up, w_down = mlp_w_up.astype(BF16), mlp_w_down.astype(BF16)
    for i in range(depth):
        a = i // 2
        if i % 2 == 0:
            kd = NSA_KV_HEADS * HEAD_DIM
            gate_cols = nsa_w_in[a, :, qd + 6 * kd:].astype(BF16)
            wg = jnp.pad(gate_cols, ((0, 0), (0, LANES - gate_cols.shape[1])))
            q, kvc, kvb, gate = _pre_mixer("nsa", x2, mod_rows, norm_rows, i, batch, seq,
                                           (nsa_in, a, wg), cos_t, sin_t)
            kvcc = _compress(kvc, batch, seq, nsa_cmp_pe[a], nsa_phi_w1[a], nsa_phi_b1[a],
                             nsa_phi_w2[a], nsa_phi_b2[a], ncp)
            o = _nsa_attention(q, kvb, kvcc, gate, batch, seq, n_cmp, n_sel)
            wo = nsa_out
        else:
            q, kvb = _pre_mixer("swa", x2, mod_rows, norm_rows, i, batch, seq, (swa_in, a), cos_t, sin_t)
            o = _swa_attention(q, kvb, swa_sinks[a], batch, seq)
            wo = swa_out
        x2 = _post_mixer(o, x2, mod_rows, norm_rows, i, batch, seq, wo, a, w_up, w_down)
    return x2.reshape(batch, seq, d)
```
